```python
import jax, jax.numpy as jnp
from jax import lax
import numpy as np


D_MODEL = 1024
BATCH = 8
SEQ = 2048
DEPTH = 1

GRID_W = 64
CTX_LEN = 256

MLA_HEADS = 8
MLA_Q_RANK = 256
MLA_KV_RANK = 256
MLA_NOPE = 64
MLA_ROPE = 32
MLA_V = 64
MLA_SCALE = (MLA_NOPE + MLA_ROPE) ** -0.5
Q_BLOCK = 128

RET_HEADS = 4
RET_DK = 128
RET_DV = 128
RET_CHUNK = 128

D_FF = 2816
CONV_WIDTH = 3

ROPE_BASE = 10000.0
EPS = 1e-6

MLA_WIDTH = MLA_HEADS * MLA_V
RET_WIDTH = RET_HEADS * RET_DV
MIX_WIDTH = MLA_WIDTH + RET_WIDTH
IN_SIZES = (MLA_Q_RANK, MLA_KV_RANK, MLA_ROPE, RET_HEADS * RET_DK, RET_HEADS * RET_DK, RET_WIDTH, RET_WIDTH)
IN_SPLITS = tuple(int(v) for v in np.cumsum(IN_SIZES)[:-1])
IN_COLS = int(sum(IN_SIZES))

kernel_name = 'hybrid_mla_retention_dit_block'


def rmsnorm(x, g):
    xf = x.astype(jnp.float32)
    y = xf * lax.rsqrt(jnp.mean(xf * xf, axis=-1, keepdims=True) + EPS)
    return (y * g.astype(jnp.float32)).astype(x.dtype)


def modulate(h, shift, scale):
    return h * (1.0 + scale) + shift


def rope_angles(pos, dim):
    inv = ROPE_BASE ** (-jnp.arange(0, dim, 2, dtype=jnp.float32) / dim)
    return pos.astype(jnp.float32)[:, None] * inv[None, :]


def apply_rope(x, ang):
    x1, x2 = jnp.split(x, 2, axis=-1)
    cos = jnp.cos(ang)[None, :, None, :].astype(x.dtype)
    sin = jnp.sin(ang)[None, :, None, :].astype(x.dtype)
    return jnp.concatenate([x1 * cos - x2 * sin, x1 * sin + x2 * cos], axis=-1)


def axial_rope(x, pos):
    pos_row, pos_col = pos
    half = x.shape[-1] // 2
    xr = apply_rope(x[..., :half], rope_angles(pos_row, half))
    xc = apply_rope(x[..., half:], rope_angles(pos_col, half))
    return jnp.concatenate([xr, xc], axis=-1)


def split_proj(p):
    return jnp.split(p, IN_SPLITS, axis=-1)


def mla_queries(c_q, g_q, w_uq, pos):
    B, S, _ = c_q.shape
    q = (rmsnorm(c_q, g_q) @ w_uq).reshape(B, S, MLA_HEADS, MLA_NOPE + MLA_ROPE)
    q_nope, q_pe = q[..., :MLA_NOPE], q[..., MLA_NOPE:]
    if pos is not None:
        q_pe = axial_rope(q_pe, pos)
    return jnp.concatenate([q_nope, q_pe], axis=-1)


def mla_keys(c_kv, k_pe, g_kv, w_ukv, pos):
    B, S, _ = c_kv.shape
    kv = (rmsnorm(c_kv, g_kv) @ w_ukv).reshape(B, S, MLA_HEADS, MLA_NOPE + MLA_V)
    k_nope, v = kv[..., :MLA_NOPE], kv[..., MLA_NOPE:]
    k_pe = k_pe[:, :, None, :]
    if pos is not None:
        k_pe = axial_rope(k_pe, pos)
    k_pe = jnp.broadcast_to(k_pe, (B, S, MLA_HEADS, MLA_ROPE))
    return jnp.concatenate([k_nope, k_pe], axis=-1), v


def softmax_attention(q, k, v):
    s = jnp.einsum('bqhd,bkhd->bhqk', q, k).astype(jnp.float32) * MLA_SCALE
    p = jax.nn.softmax(s, axis=-1).astype(v.dtype)
    return jnp.einsum('bhqk,bkhd->bqhd', p, v)


def latent_attention(q, k_lat, v_lat, k_ctx, v_ctx):
    k = jnp.concatenate([k_lat, k_ctx], axis=1)
    v = jnp.concatenate([v_lat, v_ctx], axis=1)
    B, S, H, dq = q.shape
    qb = q.reshape(B, S // Q_BLOCK, Q_BLOCK, H, dq).swapaxes(0, 1)
    o = lax.map(lambda qi: softmax_attention(qi, k, v), qb)
    return o.swapaxes(0, 1).reshape(B, S, H * MLA_V)


def retention_inputs(rq, rk, rv, pos):
    B, S, _ = rq.shape
    q = rq.reshape(B, S, RET_HEADS, RET_DK)
    k = rk.reshape(B, S, RET_HEADS, RET_DK) * (RET_DK ** -0.5)
    v = rv.reshape(B, S, RET_HEADS, RET_DV)
    if pos is not None:
        ang = rope_angles(pos, RET_DK)
        q = apply_rope(q, ang)
        k = apply_rope(k, ang)
    return q, k, v


def retention_final_state(k, v, log_gamma, reverse):
    L = k.shape[1]
    j = jnp.arange(L, dtype=jnp.float32)
    expo = j if reverse else (L - 1.0 - j)
    w = jnp.exp(log_gamma[:, None] * expo[None, :])
    return jnp.einsum('bjhd,hj,bjhe->bhde', k.astype(jnp.float32), w, v.astype(jnp.float32))


def retention_chunkwise(q, k, v, log_gamma, init_state, strict):
    B, S, H, dk = q.shape
    dv = v.shape[-1]
    C = RET_CHUNK
    n = S // C
    idx = jnp.arange(C, dtype=jnp.float32)
    diff = idx[:, None] - idx[None, :]
    mask = (diff > 0) if strict else (diff >= 0)
    expo = jnp.where(mask, diff, 0.0)
    lg = log_gamma.astype(jnp.float32)
    decay_in = jnp.where(mask[None], jnp.exp(lg[:, None, None] * expo[None]), 0.0)
    xi = jnp.exp(lg[:, None] * (idx + 1.0)[None, :])[None, :, :, None]
    zeta = jnp.exp(lg[:, None] * (C - 1.0 - idx)[None, :])[None, :, :, None]
    g_chunk = jnp.exp(lg * C)[None, :, None, None]

    def to_chunks(a):
        return a.astype(jnp.float32).reshape(B, n, C, H, a.shape[-1]).transpose(1, 0, 3, 2, 4)

    def step(state, inp):
        qi, ki, vi = inp
        att = jnp.einsum('bhqd,bhkd->bhqk', qi, ki) * decay_in[None]
        inner = jnp.einsum('bhqk,bhke->bhqe', att, vi)
        cross = jnp.einsum('bhqd,bhde->bhqe', qi, state) * xi
        new_state = state * g_chunk + jnp.einsum('bhkd,bhke->bhde', ki * zeta, vi)
        return new_state, inner + cross

    _, out = lax.scan(step, init_state.astype(jnp.float32), (to_chunks(q), to_chunks(k), to_chunks(v)))
    return out.transpose(1, 0, 3, 2, 4).reshape(B, S, H, dv)


def head_groupnorm(y, g):
    B, S, H, dv = y.shape
    mu = jnp.mean(y, axis=-1, keepdims=True)
    var = jnp.mean(jnp.square(y - mu), axis=-1, keepdims=True)
    yn = ((y - mu) * lax.rsqrt(var + EPS)).reshape(B, S, H * dv)
    return yn * g.astype(jnp.float32)


def retention_bidir(q, k, v, gate, log_gamma, state_f, state_b, g_ret):
    flip = lambda a: jnp.flip(a, axis=1)
    y_f = retention_chunkwise(q, k, v, log_gamma[0], state_f, False)
    y_b = flip(retention_chunkwise(flip(q), flip(k), flip(v), log_gamma[1], state_b, True))
    y = head_groupnorm(y_f + y_b, g_ret)
    return (jax.nn.silu(gate.astype(jnp.float32)) * y).astype(v.dtype)


def conv_ffn(h, w_up, conv_w, conv_b, w_down):
    u = h @ w_up
    u = lax.conv_general_dilated(u, conv_w[:, None, :].astype(u.dtype), window_strides=(1,),
                                 padding=((CONV_WIDTH // 2, CONV_WIDTH // 2),),
                                 dimension_numbers=('NWC', 'WIO', 'NWC'),
                                 feature_group_count=u.shape[-1]) + conv_b
    a, g = jnp.split(u, 2, axis=-1)
    return (jax.nn.silu(g) * a) @ w_down


def setup_inputs(seed: int = 0) -> dict:
    key = jax.random.key(seed)
    ks = jax.random.split(key, 21)
    nrm = lambda k, shape, scale: jax.random.normal(k, shape, jnp.float32) * scale
    h = jnp.arange(RET_HEADS, dtype=jnp.float32)
    decay_init = jnp.log(-jnp.log(1.0 - 2.0 ** (-5.0 - h)))
    return {
        'x': nrm(ks[0], (BATCH, SEQ, D_MODEL), 1.0),
        'c': nrm(ks[1], (BATCH, D_MODEL), 1.0),
        'ctx': nrm(ks[2], (BATCH, CTX_LEN, D_MODEL), 1.0),
        'c_ctx': nrm(ks[3], (D_MODEL,), 1.0),
        'w_ada': nrm(ks[4], (DEPTH, D_MODEL, 6 * D_MODEL), 0.5 * D_MODEL ** -0.5),
        'b_ada': nrm(ks[5], (DEPTH, 6 * D_MODEL), 0.02),
        'g_norm1': 1.0 + nrm(ks[6], (DEPTH, D_MODEL), 0.02),
        'w_in': nrm(ks[7], (DEPTH, D_MODEL, IN_COLS), D_MODEL ** -0.5),
        'g_q': 1.0 + nrm(ks[8], (DEPTH, MLA_Q_RANK), 0.02),
        'w_uq': nrm(ks[9], (DEPTH, MLA_Q_RANK, MLA_HEADS * (MLA_NOPE + MLA_ROPE)), MLA_Q_RANK ** -0.5),
        'g_kv': 1.0 + nrm(ks[10], (DEPTH, MLA_KV_RANK), 0.02),
        'w_ukv': nrm(ks[11], (DEPTH, MLA_KV_RANK, MLA_HEADS * (MLA_NOPE + MLA_V)), MLA_KV_RANK ** -0.5),
        'ret_decay': decay_init[None, None, :] + nrm(ks[12], (DEPTH, 2, RET_HEADS), 0.05),
        'g_ret': 1.0 + nrm(ks[13], (DEPTH, RET_WIDTH), 0.02),
        'w_out': nrm(ks[14], (DEPTH, MIX_WIDTH, D_MODEL), MIX_WIDTH ** -0.5),
        'g_norm2': 1.0 + nrm(ks[15], (DEPTH, D_MODEL), 0.02),
        'w_up': nrm(ks[16], (DEPTH, D_MODEL, 2 * D_FF), D_MODEL ** -0.5),
        'conv_w': nrm(ks[17], (DEPTH, CONV_WIDTH, 2 * D_FF), CONV_WIDTH ** -0.5),
        'conv_b': nrm(ks[18], (DEPTH, 2 * D_FF), 0.02),
        'w_down': nrm(ks[19], (DEPTH, D_FF, D_MODEL), D_FF ** -0.5),
        'g_final': 1.0 + nrm(ks[20], (D_MODEL,), 0.02),
    }


def reference(x, c, ctx, c_ctx, w_ada, b_ada, g_norm1, w_in, g_q, w_uq, g_kv, w_ukv,
              ret_decay, g_ret, w_out, g_norm2, w_up, conv_w, conv_b, w_down, g_final):
    B, S, _ = x.shape
    rows = S // GRID_W
    pos_row = jnp.repeat(jnp.arange(rows, dtype=jnp.int32), GRID_W)
    pos_col = jnp.tile(jnp.arange(GRID_W, dtype=jnp.int32), rows)
    pos_grid = (pos_row, pos_col)
    pos_seq = jnp.arange(S, dtype=jnp.int32)

    for l in range(DEPTH):
        last = l == DEPTH - 1
        mod = jax.nn.silu(c) @ w_ada[l] + b_ada[l]
        sh1, sc1, gt1, sh2, sc2, gt2 = jnp.split(mod[:, None, :], 6, axis=-1)
        mod_c = jax.nn.silu(c_ctx) @ w_ada[l] + b_ada[l]
        shc1, scc1, gtc1, shc2, scc2, gtc2 = jnp.split(mod_c, 6)
        log_gamma = -jnp.exp(ret_decay[l].astype(jnp.float32))

        h = modulate(rmsnorm(x, g_norm1[l]), sh1, sc1)
        hc = modulate(rmsnorm(ctx, g_norm1[l]), shc1, scc1)
        cq, ckv, kpe, rq, rk, rv, rg = split_proj(h @ w_in[l])
        cq_c, ckv_c, kpe_c, rq_c, rk_c, rv_c, rg_c = split_proj(hc @ w_in[l])

        k_c, v_c = mla_keys(ckv_c, kpe_c, g_kv[l], w_ukv[l], None)
        qr_c, kr_c, vr_c = retention_inputs(rq_c, rk_c, rv_c, None)
        st_f = retention_final_state(kr_c, vr_c, log_gamma[0], False)
        st_b = retention_final_state(kr_c, vr_c, log_gamma[1], True)

        q = mla_queries(cq, g_q[l], w_uq[l], pos_grid)
        k, v = mla_keys(ckv, kpe, g_kv[l], w_ukv[l], pos_grid)
        o_mla = latent_attention(q, k, v, k_c, v_c)
        qr, kr, vr = retention_inputs(rq, rk, rv, pos_seq)
        o_ret = retention_bidir(qr, kr, vr, rg, log_gamma, st_f, st_b, g_ret[l])
        o = jnp.concatenate([o_mla, o_ret.astype(o_mla.dtype)], axis=-1) @ w_out[l]
        x_new = x + gt1 * o
        x_new = x_new + gt2 * conv_ffn(modulate(rmsnorm(x_new, g_norm2[l]), sh2, sc2),
                                       w_up[l], conv_w[l], conv_b[l], w_down[l])

        if not last:
            q_c = mla_queries(cq_c, g_q[l], w_uq[l], None)
            o_mla_c = softmax_attention(q_c, k_c, v_c).reshape(B, ctx.shape[1], MLA_WIDTH)
            zero_state = jnp.zeros_like(st_f)
            o_ret_c = retention_bidir(qr_c, kr_c, vr_c, rg_c, log_gamma, zero_state, zero_state, g_ret[l])
            o_c = jnp.concatenate([o_mla_c, o_ret_c.astype(o_mla_c.dtype)], axis=-1) @ w_out[l]
            ctx = ctx + gtc1 * o_c
            ctx = ctx + gtc2 * conv_ffn(modulate(rmsnorm(ctx, g_norm2[l]), shc2, scc2),
                                        w_up[l], conv_w[l], conv_b[l], w_down[l])
        x = x_new

    return rmsnorm(x, g_final)
```

```python
import functools

import jax
import jax.numpy as jnp
import numpy as np
from jax import lax
from jax.experimental import pallas as pl
from jax.experimental.pallas import tpu as pltpu

F32 = jnp.float32
BF16 = jnp.bfloat16

GRID_W = 64
MLA_HEADS = 8
MLA_NOPE = 64
MLA_ROPE = 32
MLA_V = 64
MLA_SCALE = (MLA_NOPE + MLA_ROPE) ** -0.5
RET_HEADS = 4
RET_DK = 128
RET_DV = 128
ROPE_BASE = 10000.0
EPS = 1e-6

LANES = 128
MXU_DIM = 256
VMEM_LIMIT_BYTES = 56 * 1024 * 1024

HEAD_PAD = LANES
PRE_TM = 512
ATT_TQ = 512
RET_CHUNK = 256
OUT_TM = 512
FFN_TM = 512
FFN_FC = 256
HALO = 8

NT_DIMS = (((1,), (1,)), ((), ()))
TN_DIMS = (((0,), (0,)), ((), ()))


def _dot(a, b):
    return jnp.dot(a, b, preferred_element_type=F32)


def _rmsnorm(x, g):
    return x * lax.rsqrt(jnp.mean(x * x, axis=-1, keepdims=True) + EPS) * g


def _silu(x):
    return x * (1.0 / (1.0 + jnp.exp(-x)))


def _params(*semantics):
    return pltpu.CompilerParams(dimension_semantics=semantics, vmem_limit_bytes=VMEM_LIMIT_BYTES)


def _const_spec(shape):
    zeros = (0,) * len(shape)
    return pl.BlockSpec(shape, lambda *_: zeros, pipeline_mode=pl.Buffered(1))


def _adaln_kernel(c_ref, w_ref, b_ref, o_ref):
    a = _silu(c_ref[...])
    o_ref[...] = jnp.dot(a, w_ref[...], preferred_element_type=F32,
                         precision=lax.Precision.HIGHEST) + b_ref[...]


def _adaln(cs, w, b):
    rows, d = cs.shape
    n = w.shape[1]
    tn = 1024
    return pl.pallas_call(
        _adaln_kernel,
        out_shape=jax.ShapeDtypeStruct((rows, n), F32),
        grid=(n // tn,),
        in_specs=[pl.BlockSpec((rows, d), lambda j: (0, 0)),
                  pl.BlockSpec((d, tn), lambda j: (0, j)),
                  pl.BlockSpec((1, tn), lambda j: (0, j))],
        out_specs=pl.BlockSpec((rows, tn), lambda j: (0, j)),
        compiler_params=_params("parallel"),
        name="adaln",
    )(cs, w, b)


COL_CQ = 0
COL_CKV = 256
COL_KPE = 512
COL_KPE_ROT = 640
COL_RQ = 768
COL_RK = 1280
COL_RV = 1792
COL_RG = 2304
COLS_IN = 2816
RET_W = RET_HEADS * RET_DK


def _preproj_kernel(with_queries, x_ref, sh_ref, sc_ref, g1_ref, win_ref, gq_ref, wuq_ref,
                    gkv_ref, wk_ref, wv_ref, cm_ref, sm_ref, cr_ref, sr_ref, *out_refs):
    if with_queries:
        q_ref, k_ref, v_ref, rq_ref, rk_ref, rv_ref, sg_ref = out_refs
    else:
        k_ref, v_ref, rk_ref, rv_ref = out_refs
    x = x_ref[0]
    h = _rmsnorm(x, g1_ref[...]) * (1.0 + sc_ref[0]) + sh_ref[0]
    proj = _dot(h.astype(BF16), win_ref[...])
    cm = cm_ref[...]
    sm = sm_ref[...]
    cr = cr_ref[...]
    sr = sr_ref[...]

    ckvn = _rmsnorm(proj[:, COL_CKV:COL_CKV + 256], gkv_ref[...]).astype(BF16)
    kn = _dot(ckvn, wk_ref[...])
    v_ref[0] = _dot(ckvn, wv_ref[...]).astype(BF16)
    kpe = proj[:, COL_KPE:COL_KPE + LANES] * cm + proj[:, COL_KPE_ROT:COL_KPE_ROT + LANES] * sm
    for hd in range(MLA_HEADS):
        sl = slice(hd * HEAD_PAD, (hd + 1) * HEAD_PAD)
        k_ref[0, :, sl] = (kn[:, sl] + kpe).astype(BF16)

    for hd in range(RET_HEADS):
        sl = slice(hd * RET_DK, (hd + 1) * RET_DK)
        b = proj[:, COL_RK + hd * RET_DK:COL_RK + (hd + 1) * RET_DK]
        rk_ref[0, :, sl] = ((b * cr + pltpu.roll(b, RET_DK // 2, 1) * sr) * (RET_DK ** -0.5)).astype(BF16)
    rv_ref[0] = proj[:, COL_RV:COL_RV + RET_W].astype(BF16)

    if with_queries:
        cqn = _rmsnorm(proj[:, COL_CQ:COL_CQ + 256], gq_ref[...]).astype(BF16)
        qa = _dot(cqn, wuq_ref[...])
        rot0 = MLA_HEADS * HEAD_PAD
        for hd in range(MLA_HEADS):
            sl = slice(hd * HEAD_PAD, (hd + 1) * HEAD_PAD)
            a = qa[:, sl]
            b = qa[:, rot0 + hd * HEAD_PAD:rot0 + (hd + 1) * HEAD_PAD]
            q_ref[0, :, sl] = ((a * cm + b * sm) * MLA_SCALE).astype(BF16)
        for hd in range(RET_HEADS):
            sl = slice(hd * RET_DK, (hd + 1) * RET_DK)
            a = proj[:, COL_RQ + hd * RET_DK:COL_RQ + (hd + 1) * RET_DK]
            rq_ref[0, :, sl] = (a * cr + pltpu.roll(a, RET_DK // 2, 1) * sr).astype(BF16)
        sg_ref[0] = _silu(proj[:, COL_RG:COL_RG + RET_W]).astype(BF16)


def _preproj(x, sh, sc, g1, win, gq, wuq, gkv, wk, wv, cm, sm, cr, sr, *, with_queries):
    bsz, seq, d = x.shape
    tm = min(PRE_TM, seq)
    qk_w = MLA_HEADS * HEAD_PAD
    v_w = MLA_HEADS * MLA_V

    def tok(width):
        return pl.BlockSpec((1, tm, width), lambda b, i: (b, i, 0))

    def out(width):
        return jax.ShapeDtypeStruct((bsz, seq, width), BF16)

    mod_spec = pl.BlockSpec((1, 1, d), lambda b, i: (b, 0, 0))
    tab_spec = pl.BlockSpec((tm, LANES), lambda b, i: (i, 0))
    if with_queries:
        out_shape = [out(qk_w), out(qk_w), out(v_w), out(RET_W), out(RET_W), out(RET_W), out(RET_W)]
        out_specs = [tok(qk_w), tok(qk_w), tok(v_w), tok(RET_W), tok(RET_W), tok(RET_W), tok(RET_W)]
    else:
        out_shape = [out(qk_w), out(v_w), out(RET_W), out(RET_W)]
        out_specs = [tok(qk_w), tok(v_w), tok(RET_W), tok(RET_W)]
    return pl.pallas_call(
        functools.partial(_preproj_kernel, with_queries),
        out_shape=out_shape,
        grid=(bsz, seq // tm),
        in_specs=[tok(d), mod_spec, mod_spec, _const_spec(g1.shape), _const_spec(win.shape),
                  _const_spec(gq.shape), _const_spec(wuq.shape), _const_spec(gkv.shape),
                  _const_spec(wk.shape), _const_spec(wv.shape), tab_spec, tab_spec, tab_spec, tab_spec],
        out_specs=out_specs,
        compiler_params=_params("parallel", "parallel"),
        name="preproj_q" if with_queries else "preproj_ctx",
    )(x, sh, sc, g1, win, gq, wuq, gkv, wk, wv, cm, sm, cr, sr)


def _mla_attn_kernel(q_ref, kl_ref, kc_ref, vl_ref, vc_ref, o_ref):
    outs = []
    for j in range(2):
        sl = slice(j * HEAD_PAD, (j + 1) * HEAD_PAD)
        qh = q_ref[0, :, sl]
        s1 = lax.dot_general(qh, kl_ref[0, :, sl], NT_DIMS, preferred_element_type=F32)
        s2 = lax.dot_general(qh, kc_ref[0, :, sl], NT_DIMS, preferred_element_type=F32)
        m = jnp.maximum(jnp.max(s1, axis=-1, keepdims=True), jnp.max(s2, axis=-1, keepdims=True))
        p1 = jnp.exp(s1 - m)
        p2 = jnp.exp(s2 - m)
        l = jnp.sum(p1, axis=-1, keepdims=True) + jnp.sum(p2, axis=-1, keepdims=True)
        o = _dot(p1.astype(BF16), vl_ref[0]) + _dot(p2.astype(BF16), vc_ref[0])
        outs.append(o * (1.0 / l))
    lane = lax.broadcasted_iota(jnp.int32, outs[0].shape, 1)
    o_ref[0] = jnp.where(lane < MLA_V, outs[0], outs[1]).astype(BF16)


def _mla_attn(q, k_lat, k_ctx, v_lat, v_ctx):
    bsz, seq, _ = q.shape
    ctx_len = k_ctx.shape[1]
    tq = min(ATT_TQ, seq)
    pairs = MLA_HEADS // 2
    return pl.pallas_call(
        _mla_attn_kernel,
        out_shape=jax.ShapeDtypeStruct((bsz, seq, MLA_HEADS * MLA_V), BF16),
        grid=(bsz, pairs, seq // tq),
        in_specs=[pl.BlockSpec((1, tq, 2 * HEAD_PAD), lambda b, p, i: (b, i, p)),
                  pl.BlockSpec((1, seq, 2 * HEAD_PAD), lambda b, p, i: (b, 0, p)),
                  pl.BlockSpec((1, ctx_len, 2 * HEAD_PAD), lambda b, p, i: (b, 0, p)),
                  pl.BlockSpec((1, seq, 2 * MLA_V), lambda b, p, i: (b, 0, p)),
                  pl.BlockSpec((1, ctx_len, 2 * MLA_V), lambda b, p, i: (b, 0, p))],
        out_specs=pl.BlockSpec((1, tq, 2 * MLA_V), lambda b, p, i: (b, i, p)),
        compiler_params=_params("parallel", "parallel", "parallel"),
        name="mla_attn",
    )(q, k_lat, k_ctx, v_lat, v_ctx)


def _retention_kernel(rd_ref, q_ref, k_ref, v_ref, sg_ref, kc_ref, vc_ref, g_ref, o_ref, pf_ref, sb_ref):
    c = RET_CHUNK
    n_chunks = q_ref.shape[1] // c
    lgf = -jnp.exp(rd_ref[0, 0:1, :])
    lgb = -jnp.exp(rd_ref[0, 1:2, :])
    lgf1 = lgf[:, :RET_DV]
    lgb1 = lgb[:, :RET_DV]
    row = lax.broadcasted_iota(jnp.int32, (c, RET_DV), 0).astype(F32)
    zeta_f = jnp.exp(lgf1 * (c - 1.0 - row))
    zeta_b = jnp.exp(lgb1 * row)
    xi_f = jnp.exp(lgf1 * (row + 1.0))
    xi_b = jnp.exp(lgb1 * (c - row))
    gc_f = jnp.exp(lgf1 * float(c))
    gc_b = jnp.exp(lgb1 * float(c))
    diff = (lax.broadcasted_iota(jnp.int32, (c, c), 0) - lax.broadcasted_iota(jnp.int32, (c, c), 1)).astype(F32)
    dmat = jnp.where(diff >= 0.0, jnp.exp(lgf * jnp.maximum(diff, 0.0)), jnp.exp(lgb * jnp.maximum(-diff, 0.0)))

    def kv_outer(kk, vv):
        vf = vv.astype(F32)
        v2 = jnp.concatenate([vf * zeta_f, vf * zeta_b], axis=1).astype(BF16)
        return lax.dot_general(kk, v2, TN_DIMS, preferred_element_type=F32)

    p_ctx = kv_outer(kc_ref[0], vc_ref[0])
    sf = p_ctx[:, :RET_DV]
    sb = p_ctx[:, RET_DV:]

    for ci in reversed(range(n_chunks)):
        rows = slice(ci * c, (ci + 1) * c)
        p = kv_outer(k_ref[0, rows, :], v_ref[0, rows, :])
        pf_ref[ci] = p[:, :RET_DV]
        sb_ref[ci] = sb
        sb = gc_b * sb + p[:, RET_DV:]

    g = g_ref[...]
    for ci in range(n_chunks):
        rows = slice(ci * c, (ci + 1) * c)
        qc = q_ref[0, rows, :]
        s = lax.dot_general(qc, k_ref[0, rows, :], NT_DIMS, preferred_element_type=F32)
        inner = _dot((s * dmat).astype(BF16), v_ref[0, rows, :])
        st = jnp.concatenate([sf, sb_ref[ci]], axis=1).astype(BF16)
        cross = _dot(qc, st)
        y = inner + cross[:, :RET_DV] * xi_f + cross[:, RET_DV:] * xi_b
        mu = jnp.mean(y, axis=-1, keepdims=True)
        d = y - mu
        var = jnp.mean(d * d, axis=-1, keepdims=True)
        yn = d * lax.rsqrt(var + EPS) * g
        o_ref[0, rows, :] = (sg_ref[0, rows, :].astype(F32) * yn).astype(BF16)
        sf = gc_f * sf + pf_ref[ci]


def _retention(rd, rq, rk, rv, sg, rk_ctx, rv_ctx, g_ret):
    bsz, seq, _ = rq.shape
    ctx_len = rk_ctx.shape[1]
    assert ctx_len == RET_CHUNK and seq % RET_CHUNK == 0

    def head(rows):
        return pl.BlockSpec((1, rows, RET_DK), lambda b, h: (b, 0, h))

    return pl.pallas_call(
        _retention_kernel,
        out_shape=jax.ShapeDtypeStruct((bsz, seq, RET_HEADS * RET_DV), BF16),
        grid=(bsz, RET_HEADS),
        in_specs=[pl.BlockSpec((1, 2, RET_CHUNK), lambda b, h: (h, 0, 0)),
                  head(seq), head(seq), head(seq), head(seq), head(ctx_len), head(ctx_len),
                  pl.BlockSpec((1, RET_DV), lambda b, h: (0, h))],
        out_specs=head(seq),
        scratch_shapes=[pltpu.VMEM((seq // RET_CHUNK, RET_DK, RET_DV), F32),
                        pltpu.VMEM((seq // RET_CHUNK, RET_DK, RET_DV), F32)],
        compiler_params=_params("parallel", "parallel"),
        name="retention",
    )(rd, rq, rk, rv, sg, rk_ctx, rv_ctx, g_ret)


def _outproj_kernel(om_ref, or_ref, x_ref, gt_ref, w_ref, o_ref):
    half = om_ref.shape[2]
    acc = _dot(om_ref[0], w_ref[0:half, :]) + _dot(or_ref[0], w_ref[half:, :])
    o_ref[0] = x_ref[0] + gt_ref[0] * acc


def _outproj(o_mla, o_ret, x, gt, w):
    bsz, seq, d = x.shape
    tm = min(OUT_TM, seq)

    def tok(width):
        return pl.BlockSpec((1, tm, width), lambda b, i: (b, i, 0))

    return pl.pallas_call(
        _outproj_kernel,
        out_shape=jax.ShapeDtypeStruct((bsz, seq, d), F32),
        grid=(bsz, seq // tm),
        in_specs=[tok(o_mla.shape[2]), tok(o_ret.shape[2]), tok(d),
                  pl.BlockSpec((1, 1, d), lambda b, i: (b, 0, 0)), _const_spec(w.shape)],
        out_specs=tok(d),
        compiler_params=_params("parallel", "parallel"),
        name="outproj",
    )(o_mla, o_ret, x, gt, w)


def _ffn_kernel(x_ref, xp_ref, xn_ref, sh_ref, sc_ref, gt_ref, g2_ref, gf_ref,
                wup_ref, cw_ref, cb_ref, wdn_ref, o_ref, hs_ref, acc_ref):
    tm = x_ref.shape[1]
    n_inner = wup_ref.shape[0]
    i = pl.program_id(1)
    x = x_ref[0]
    g2 = g2_ref[...]
    scale = 1.0 + sc_ref[0]
    shift = sh_ref[0]
    hs_ref[0:tm, :] = (_rmsnorm(x, g2) * scale + shift).astype(BF16)
    halo = jnp.concatenate([xp_ref[0], xn_ref[0]], axis=0)
    hs_ref[tm:tm + 2 * HALO, :] = (_rmsnorm(halo, g2) * scale + shift).astype(BF16)
    has_prev = (i > 0).astype(F32)
    has_next = (i < pl.num_programs(1) - 1).astype(F32)
    row = lax.broadcasted_iota(jnp.int32, (tm, 2 * FFN_FC), 0)
    acc_ref[...] = jnp.zeros_like(acc_ref)

    def body(j, carry):
        u = _dot(hs_ref[...], wup_ref[j])
        um = u[0:tm]
        u_before = u[tm + HALO - 1:tm + HALO] * has_prev
        u_after = u[tm + HALO:tm + HALO + 1] * has_next
        up = jnp.where(row == 0, u_before, pltpu.roll(um, 1, 0))
        un = jnp.where(row == tm - 1, u_after, pltpu.roll(um, tm - 1, 0))
        cw = cw_ref[j]
        cv = up * cw[0:1] + um * cw[1:2] + un * cw[2:3] + cb_ref[j]
        act = (_silu(cv[:, FFN_FC:]) * cv[:, :FFN_FC]).astype(BF16)
        acc_ref[...] += _dot(act, wdn_ref[j])
        return carry

    lax.fori_loop(0, n_inner, body, 0)
    o_ref[0] = _rmsnorm(x + gt_ref[0] * acc_ref[...], gf_ref[...])


def _ffn(x, sh, sc, gt, g2, gf, wup, cw, cb, wdn):
    bsz, seq, d = x.shape
    tm = min(FFN_TM, seq)
    blocks_per_tile = tm // HALO
    last_block = seq // HALO - 1
    mod_spec = pl.BlockSpec((1, 1, d), lambda b, i: (b, 0, 0))
    return pl.pallas_call(
        _ffn_kernel,
        out_shape=jax.ShapeDtypeStruct((bsz, seq, d), F32),
        grid=(bsz, seq // tm),
        in_specs=[pl.BlockSpec((1, tm, d), lambda b, i: (b, i, 0)),
                  pl.BlockSpec((1, HALO, d), lambda b, i: (b, jnp.maximum(i * blocks_per_tile - 1, 0), 0)),
                  pl.BlockSpec((1, HALO, d), lambda b, i: (b, jnp.minimum((i + 1) * blocks_per_tile, last_block), 0)),
                  mod_spec, mod_spec, mod_spec, _const_spec(g2.shape), _const_spec(gf.shape),
                  _const_spec(wup.shape), _const_spec(cw.shape), _const_spec(cb.shape), _const_spec(wdn.shape)],
        out_specs=pl.BlockSpec((1, tm, d), lambda b, i: (b, i, 0)),
        scratch_shapes=[pltpu.VMEM((tm + 2 * HALO, d), BF16), pltpu.VMEM((tm, d), F32)],
        compiler_params=_params("parallel", "parallel"),
        name="ffn",
    )(x, x, x, sh, sc, gt, g2, gf, wup, cw, cb, wdn)


def _rot_half_cols(w, half):
    shape = w.shape
    w = w.reshape(shape[:-1] + (shape[-1] // (2 * half), 2, half))
    return jnp.stack([-w[..., 1, :], w[..., 0, :]], axis=-2).reshape(shape)


def _rope_angles(pos, dim):
    inv = ROPE_BASE ** (-jnp.arange(0, dim, 2, dtype=F32) / dim)
    return pos.astype(F32)[:, None] * inv[None, :]


def _mla_tables(seq, with_pos):
    if not with_pos:
        return jnp.ones((seq, LANES), F32), jnp.zeros((seq, LANES), F32)
    pos = jnp.arange(seq, dtype=jnp.int32)
    quarter = MLA_ROPE // 2
    ar = _rope_angles(pos // GRID_W, quarter)
    ac = _rope_angles(pos % GRID_W, quarter)
    ang = jnp.concatenate([ar, ar, ac, ac], axis=-1)
    ones = jnp.ones((seq, MLA_NOPE), F32)
    pad = LANES - MLA_NOPE - MLA_ROPE
    cos = jnp.concatenate([ones, jnp.cos(ang), jnp.ones((seq, pad), F32)], axis=-1)
    sin = jnp.concatenate([0.0 * ones, jnp.sin(ang), jnp.zeros((seq, pad), F32)], axis=-1)
    return cos, sin


def _ret_tables(seq, with_pos):
    if not with_pos:
        return jnp.ones((seq, RET_DK), F32), jnp.zeros((seq, RET_DK), F32)
    ang = _rope_angles(jnp.arange(seq, dtype=jnp.int32), RET_DK)
    return (jnp.concatenate([jnp.cos(ang), jnp.cos(ang)], axis=-1),
            jnp.concatenate([-jnp.sin(ang), jnp.sin(ang)], axis=-1))


def _pad_heads(nope, rope):
    r = nope.shape[0]
    pad = jnp.zeros((r, MLA_HEADS, HEAD_PAD - MLA_NOPE - MLA_ROPE), nope.dtype)
    return jnp.concatenate([nope, rope, pad], axis=-1).reshape(r, MLA_HEADS * HEAD_PAD)


def _layout_weights(w_in, w_uq, w_ukv, w_up, conv_w, conv_b, w_down):
    d = w_in.shape[0]
    q_rank = w_uq.shape[0]
    o = np.cumsum([0, q_rank, w_ukv.shape[0], MLA_ROPE, RET_W, RET_W, RET_W, RET_W])
    w_cq, w_ckv, w_kpe, w_rq, w_rk, w_rv, w_rg = [w_in[:, o[t]:o[t + 1]] for t in range(7)]
    rot_half = MLA_ROPE // 4

    def place_rope(w):
        z0 = jnp.zeros((d, MLA_NOPE), w.dtype)
        z1 = jnp.zeros((d, LANES - MLA_NOPE - MLA_ROPE), w.dtype)
        return jnp.concatenate([z0, w, z1], axis=-1)

    win = jnp.concatenate([w_cq, w_ckv, place_rope(w_kpe), place_rope(_rot_half_cols(w_kpe, rot_half)),
                           w_rq, w_rk, w_rv, w_rg], axis=-1).astype(BF16)
    assert win.shape[1] == COLS_IN

    uq = w_uq.reshape(q_rank, MLA_HEADS, MLA_NOPE + MLA_ROPE)
    uq_n, uq_r = uq[..., :MLA_NOPE], uq[..., MLA_NOPE:]
    wuq = jnp.concatenate([_pad_heads(uq_n, uq_r),
                           _pad_heads(jnp.zeros_like(uq_n), _rot_half_cols(uq_r, rot_half))],
                          axis=-1).astype(BF16)

    ukv = w_ukv.reshape(w_ukv.shape[0], MLA_HEADS, MLA_NOPE + MLA_V)
    wk = _pad_heads(ukv[..., :MLA_NOPE], jnp.zeros(ukv.shape[:2] + (MLA_ROPE,), ukv.dtype)).astype(BF16)
    wv = ukv[..., MLA_NOPE:].reshape(w_ukv.shape[0], MLA_HEADS * MLA_V).astype(BF16)

    d_ff = w_down.shape[0]
    nj = d_ff // FFN_FC

    def pair_chunks(a):
        lead = a.shape[:-1]
        a = a.reshape(lead + (2, nj, FFN_FC))
        a = jnp.moveaxis(a, -2, 0)
        return a.reshape((nj,) + lead + (2 * FFN_FC,))

    wup = pair_chunks(w_up).astype(BF16)
    cw = pair_chunks(conv_w)
    cb = pair_chunks(conv_b[None, :])
    wdn = w_down.reshape(nj, FFN_FC, w_down.shape[1]).astype(BF16)
    return win, wuq, wk, wv, wup, cw, cb, wdn


def kernel(x, c, ctx, c_ctx, w_ada, b_ada, g_norm1, w_in, g_q, w_uq, g_kv, w_ukv, ret_decay, g_ret,
           w_out, g_norm2, w_up, conv_w, conv_b, w_down, g_final):
    bsz, seq, d = x.shape
    ctx_len = ctx.shape[1]
    depth = w_ada.shape[0]
    assert depth == 1, "single-layer block"
    l = 0

    rows = -(-(bsz + 1) // 8) * 8
    cs = jnp.concatenate([c, c_ctx[None, :], jnp.zeros((rows - bsz - 1, d), F32)], axis=0)
    mod = _adaln(cs, w_ada[l], b_ada[l][None, :])
    sh1, sc1, gt1, sh2, sc2, gt2 = [mod[:bsz, t * d:(t + 1) * d][:, None, :] for t in range(6)]
    shc1 = jnp.broadcast_to(mod[bsz, 0:d][None, None, :], (bsz, 1, d))
    scc1 = jnp.broadcast_to(mod[bsz, d:2 * d][None, None, :], (bsz, 1, d))

    win, wuq, wk, wv, wup, cw, cb, wdn = _layout_weights(w_in[l], w_uq[l], w_ukv[l], w_up[l], conv_w[l],
                                                       conv_b[l], w_down[l])
    g1 = g_norm1[l][None, :]
    gq = g_q[l][None, :]
    gkv = g_kv[l][None, :]

    cm, sm = _mla_tables(seq, True)
    cr, sr = _ret_tables(seq, True)
    q, k, v, rq, rk, rv, sg = _preproj(x, sh1, sc1, g1, win, gq, wuq, gkv, wk, wv, cm, sm, cr, sr,
                                       with_queries=True)
    cm0, sm0 = _mla_tables(ctx_len, False)
    cr0, sr0 = _ret_tables(ctx_len, False)
    k_c, v_c, rk_c, rv_c = _preproj(ctx, shc1, scc1, g1, win, gq, wuq, gkv, wk, wv, cm0, sm0, cr0, sr0,
                                    with_queries=False)

    o_mla = _mla_attn(q, k, k_c, v, v_c)
    rd = jnp.broadcast_to(jnp.transpose(ret_decay[l])[:, :, None], (RET_HEADS, 2, RET_CHUNK)).astype(F32)
    o_ret = _retention(rd, rq, rk, rv, sg, rk_c, rv_c, g_ret[l][None, :])

    x_new = _outproj(o_mla, o_ret, x, gt1, w_out[l].astype(BF16))
    return _ffn(x_new, sh2, sc2, gt2, g_norm2[l][None, :], g_final[None, :], wup, cw, cb, wdn)
```

```python
import functools

import jax
import jax.numpy as jnp
import numpy as np
from jax import lax
from jax.experimental import pallas as pl
from jax.experimental.pallas import tpu as pltpu

F32 = jnp.float32
BF16 = jnp.bfloat16

GRID_W = 64
MLA_HEADS = 8
MLA_NOPE = 64
MLA_ROPE = 32
MLA_V = 64
MLA_SCALE = (MLA_NOPE + MLA_ROPE) ** -0.5
RET_HEADS = 4
RET_DK = 128
RET_DV = 128
ROPE_BASE = 10000.0
EPS = 1e-6

LANES = 128
MXU_DIM = 256
VMEM_LIMIT_BYTES = 56 * 1024 * 1024

HEAD_PAD = LANES
PRE_TM = 512
ATT_TQ = 512
RET_CHUNK = 256
OUT_TM = 512
FFN_TM = 512
FFN_FC = 256
HALO = 8

NT_DIMS = (((1,), (1,)), ((), ()))
TN_DIMS = (((0,), (0,)), ((), ()))


def _dot(a, b):
    return jnp.dot(a, b, preferred_element_type=F32)


def _rmsnorm(x, g):
    return x * lax.rsqrt(jnp.mean(x * x, axis=-1, keepdims=True) + EPS) * g


def _silu(x):
    return x * (1.0 / (1.0 + jnp.exp(-x)))


def _params(*semantics):
    return pltpu.CompilerParams(dimension_semantics=semantics, vmem_limit_bytes=VMEM_LIMIT_BYTES)


def _const_spec(shape):
    zeros = (0,) * len(shape)
    return pl.BlockSpec(shape, lambda *_: zeros, pipeline_mode=pl.Buffered(1))


def _adaln_kernel(c_ref, w_ref, b_ref, o_ref):
    a = _silu(c_ref[...])
    o_ref[...] = jnp.dot(a, w_ref[...], preferred_element_type=F32,
                         precision=lax.Precision.HIGHEST) + b_ref[...]


def _adaln(cs, w, b):
    rows, d = cs.shape
    n = w.shape[1]
    tn = 1024
    return pl.pallas_call(
        _adaln_kernel,
        out_shape=jax.ShapeDtypeStruct((rows, n), F32),
        grid=(n // tn,),
        in_specs=[pl.BlockSpec((rows, d), lambda j: (0, 0)),
                  pl.BlockSpec((d, tn), lambda j: (0, j)),
                  pl.BlockSpec((1, tn), lambda j: (0, j))],
        out_specs=pl.BlockSpec((rows, tn), lambda j: (0, j)),
        compiler_params=_params("parallel"),
        name="adaln",
    )(cs, w, b)


COL_CQ = 0
COL_CKV = 256
COL_KPE = 512
COL_KPE_ROT = 640
COL_RQ = 768
COL_RK = 1280
COL_RV = 1792
COL_RG = 2304
COLS_IN = 2816
RET_W = RET_HEADS * RET_DK


def _preproj_kernel(with_queries, x_ref, sh_ref, sc_ref, g1_ref, win_ref, gq_ref, wuq_ref,
                    gkv_ref, wk_ref, wv_ref, cm_ref, sm_ref, cr_ref, sr_ref, *out_refs):
    if with_queries:
        q_ref, k_ref, v_ref, rq_ref, rk_ref, rv_ref, sg_ref = out_refs
    else:
        k_ref, v_ref, rk_ref, rv_ref = out_refs
    x = x_ref[0]
    h = _rmsnorm(x, g1_ref[...]) * (1.0 + sc_ref[0]) + sh_ref[0]
    proj = _dot(h.astype(BF16), win_ref[...])
    cm = cm_ref[...]
    sm = sm_ref[...]
    cr = cr_ref[...]
    sr = sr_ref[...]

    ckvn = _rmsnorm(proj[:, COL_CKV:COL_CKV + 256], gkv_ref[...]).astype(BF16)
    kn = _dot(ckvn, wk_ref[...])
    v_ref[0] = lax.dot_general(wv_ref[...], ckvn, NT_DIMS, preferred_element_type=F32).astype(BF16)
    kpe = proj[:, COL_KPE:COL_KPE + LANES] * cm + proj[:, COL_KPE_ROT:COL_KPE_ROT + LANES] * sm
    for hd in range(MLA_HEADS):
        sl = slice(hd * HEAD_PAD, (hd + 1) * HEAD_PAD)
        k_ref[0, :, sl] = (kn[:, sl] + kpe).astype(BF16)

    for hd in range(RET_HEADS):
        sl = slice(hd * RET_DK, (hd + 1) * RET_DK)
        b = proj[:, COL_RK + hd * RET_DK:COL_RK + (hd + 1) * RET_DK]
        rk_ref[0, :, sl] = ((b * cr + pltpu.roll(b, RET_DK // 2, 1) * sr) * (RET_DK ** -0.5)).astype(BF16)
    rv_ref[0] = proj[:, COL_RV:COL_RV + RET_W].astype(BF16)

    if with_queries:
        cqn = _rmsnorm(proj[:, COL_CQ:COL_CQ + 256], gq_ref[...]).astype(BF16)
        qa = _dot(cqn, wuq_ref[...])
        rot0 = MLA_HEADS * HEAD_PAD
        for hd in range(MLA_HEADS):
            sl = slice(hd * HEAD_PAD, (hd + 1) * HEAD_PAD)
            a = qa[:, sl]
            b = qa[:, rot0 + hd * HEAD_PAD:rot0 + (hd + 1) * HEAD_PAD]
            q_ref[0, :, sl] = ((a * cm + b * sm) * MLA_SCALE).astype(BF16)
        for hd in range(RET_HEADS):
            sl = slice(hd * RET_DK, (hd + 1) * RET_DK)
            a = proj[:, COL_RQ + hd * RET_DK:COL_RQ + (hd + 1) * RET_DK]
            rq_ref[0, :, sl] = (a * cr + pltpu.roll(a, RET_DK // 2, 1) * sr).astype(BF16)
        sg_ref[0] = _silu(proj[:, COL_RG:COL_RG + RET_W]).astype(BF16)


def _preproj(x, sh, sc, g1, win, gq, wuq, gkv, wk, wv, cm, sm, cr, sr, *, with_queries):
    bsz, seq, d = x.shape
    tm = min(PRE_TM, seq)
    qk_w = MLA_HEADS * HEAD_PAD
    v_w = MLA_HEADS * MLA_V

    def tok(width):
        return pl.BlockSpec((1, tm, width), lambda b, i: (b, i, 0))

    def out(width):
        return jax.ShapeDtypeStruct((bsz, seq, width), BF16)

    mod_spec = pl.BlockSpec((1, 1, d), lambda b, i: (b, 0, 0))
    tab_spec = pl.BlockSpec((tm, LANES), lambda b, i: (i, 0))
    vt_shape = jax.ShapeDtypeStruct((bsz, v_w, seq), BF16)
    vt_spec = pl.BlockSpec((1, v_w, tm), lambda b, i: (b, 0, i))
    if with_queries:
        out_shape = [out(qk_w), out(qk_w), vt_shape, out(RET_W), out(RET_W), out(RET_W), out(RET_W)]
        out_specs = [tok(qk_w), tok(qk_w), vt_spec, tok(RET_W), tok(RET_W), tok(RET_W), tok(RET_W)]
    else:
        out_shape = [out(qk_w), vt_shape, out(RET_W), out(RET_W)]
        out_specs = [tok(qk_w), vt_spec, tok(RET_W), tok(RET_W)]
    return pl.pallas_call(
        functools.partial(_preproj_kernel, with_queries),
        out_shape=out_shape,
        grid=(bsz, seq // tm),
        in_specs=[tok(d), mod_spec, mod_spec, _const_spec(g1.shape), _const_spec(win.shape),
                  _const_spec(gq.shape), _const_spec(wuq.shape), _const_spec(gkv.shape),
                  _const_spec(wk.shape), _const_spec(wv.shape), tab_spec, tab_spec, tab_spec, tab_spec],
        out_specs=out_specs,
        compiler_params=_params("parallel", "parallel"),
        name="preproj_q" if with_queries else "preproj_ctx",
    )(x, sh, sc, g1, win, gq, wuq, gkv, wk, wv, cm, sm, cr, sr)


def _mla_attn_kernel(q_ref, kl_ref, kc_ref, vl_ref, vc_ref, o_ref):
    outs = []
    for j in range(2):
        sl = slice(j * HEAD_PAD, (j + 1) * HEAD_PAD)
        qh = q_ref[0, :, sl]
        s1 = lax.dot_general(kl_ref[0, :, sl], qh, NT_DIMS, preferred_element_type=F32)
        s2 = lax.dot_general(kc_ref[0, :, sl], qh, NT_DIMS, preferred_element_type=F32)
        m = jnp.maximum(jnp.max(s1, axis=0, keepdims=True), jnp.max(s2, axis=0, keepdims=True))
        p1 = jnp.exp(s1 - m)
        p2 = jnp.exp(s2 - m)
        l = jnp.sum(p1, axis=0, keepdims=True) + jnp.sum(p2, axis=0, keepdims=True)
        ot = _dot(vl_ref[0], p1.astype(BF16)) + _dot(vc_ref[0], p2.astype(BF16))
        outs.append(ot * (1.0 / l))
    row = lax.broadcasted_iota(jnp.int32, outs[0].shape, 0)
    o_ref[0] = jnp.where(row < MLA_V, outs[0], outs[1]).T.astype(BF16)


def _mla_attn(q, k_lat, k_ctx, v_lat, v_ctx):
    bsz, seq, _ = q.shape
    ctx_len = k_ctx.shape[1]
    tq = min(ATT_TQ, seq)
    pairs = MLA_HEADS // 2
    return pl.pallas_call(
        _mla_attn_kernel,
        out_shape=jax.ShapeDtypeStruct((bsz, seq, MLA_HEADS * MLA_V), BF16),
        grid=(bsz, pairs, seq // tq),
        in_specs=[pl.BlockSpec((1, tq, 2 * HEAD_PAD), lambda b, p, i: (b, i, p)),
                  pl.BlockSpec((1, seq, 2 * HEAD_PAD), lambda b, p, i: (b, 0, p)),
                  pl.BlockSpec((1, ctx_len, 2 * HEAD_PAD), lambda b, p, i: (b, 0, p)),
                  pl.BlockSpec((1, 2 * MLA_V, seq), lambda b, p, i: (b, p, 0)),
                  pl.BlockSpec((1, 2 * MLA_V, ctx_len), lambda b, p, i: (b, p, 0))],
        out_specs=pl.BlockSpec((1, tq, 2 * MLA_V), lambda b, p, i: (b, i, p)),
        compiler_params=_params("parallel", "parallel", "parallel"),
        name="mla_attn",
    )(q, k_lat, k_ctx, v_lat, v_ctx)


def _retention_kernel(rd_ref, q_ref, k_ref, v_ref, sg_ref, kc_ref, vc_ref, g_ref, o_ref, pf_ref, sb_ref):
    c = RET_CHUNK
    n_chunks = q_ref.shape[1] // c
    lgf = -jnp.exp(rd_ref[0, 0:1, :])
    lgb = -jnp.exp(rd_ref[0, 1:2, :])
    lgf1 = lgf[:, :RET_DV]
    lgb1 = lgb[:, :RET_DV]
    row = lax.broadcasted_iota(jnp.int32, (c, RET_DV), 0).astype(F32)
    zeta_f = jnp.exp(lgf1 * (c - 1.0 - row))
    zeta_b = jnp.exp(lgb1 * row)
    xi_f = jnp.exp(lgf1 * (row + 1.0))
    xi_b = jnp.exp(lgb1 * (c - row))
    gc_f = jnp.exp(lgf1 * float(c))
    gc_b = jnp.exp(lgb1 * float(c))
    diff = (lax.broadcasted_iota(jnp.int32, (c, c), 0) - lax.broadcasted_iota(jnp.int32, (c, c), 1)).astype(F32)
    dmat = jnp.where(diff >= 0.0, jnp.exp(lgf * jnp.maximum(diff, 0.0)), jnp.exp(lgb * jnp.maximum(-diff, 0.0)))

    def kv_outer(kk, vv):
        vf = vv.astype(F32)
        v2 = jnp.concatenate([vf * zeta_f, vf * zeta_b], axis=1).astype(BF16)
        return lax.dot_general(kk, v2, TN_DIMS, preferred_element_type=F32)

    p_ctx = kv_outer(kc_ref[0], vc_ref[0])
    sf = p_ctx[:, :RET_DV]
    sb = p_ctx[:, RET_DV:]

    for ci in reversed(range(n_chunks)):
        rows = slice(ci * c, (ci + 1) * c)
        p = kv_outer(k_ref[0, rows, :], v_ref[0, rows, :])
        pf_ref[ci] = p[:, :RET_DV]
        sb_ref[ci] = sb
        sb = gc_b * sb + p[:, RET_DV:]

    g = g_ref[...]
    for ci in range(n_chunks):
        rows = slice(ci * c, (ci + 1) * c)
        qc = q_ref[0, rows, :]
        s = lax.dot_general(qc, k_ref[0, rows, :], NT_DIMS, preferred_element_type=F32)
        inner = _dot((s * dmat).astype(BF16), v_ref[0, rows, :])
        st = jnp.concatenate([sf, sb_ref[ci]], axis=1).astype(BF16)
        cross = _dot(qc, st)
        y = inner + cross[:, :RET_DV] * xi_f + cross[:, RET_DV:] * xi_b
        mu = jnp.mean(y, axis=-1, keepdims=True)
        d = y - mu
        var = jnp.mean(d * d, axis=-1, keepdims=True)
        yn = d * lax.rsqrt(var + EPS) * g
        o_ref[0, rows, :] = (sg_ref[0, rows, :].astype(F32) * yn).astype(BF16)
        sf = gc_f * sf + pf_ref[ci]


def _retention(rd, rq, rk, rv, sg, rk_ctx, rv_ctx, g_ret):
    bsz, seq, _ = rq.shape
    ctx_len = rk_ctx.shape[1]
    assert ctx_len == RET_CHUNK and seq % RET_CHUNK == 0

    def head(rows):
        return pl.BlockSpec((1, rows, RET_DK), lambda b, h: (b, 0, h))

    return pl.pallas_call(
        _retention_kernel,
        out_shape=jax.ShapeDtypeStruct((bsz, seq, RET_HEADS * RET_DV), BF16),
        grid=(bsz, RET_HEADS),
        in_specs=[pl.BlockSpec((1, 2, RET_CHUNK), lambda b, h: (h, 0, 0)),
                  head(seq), head(seq), head(seq), head(seq), head(ctx_len), head(ctx_len),
                  pl.BlockSpec((1, RET_DV), lambda b, h: (0, h))],
        out_specs=head(seq),
        scratch_shapes=[pltpu.VMEM((seq // RET_CHUNK, RET_DK, RET_DV), F32),
                        pltpu.VMEM((seq // RET_CHUNK, RET_DK, RET_DV), F32)],
        compiler_params=_params("parallel", "parallel"),
        name="retention",
    )(rd, rq, rk, rv, sg, rk_ctx, rv_ctx, g_ret)


def _outproj_kernel(om_ref, or_ref, x_ref, gt_ref, w_ref, o_ref):
    half = om_ref.shape[2]
    acc = _dot(om_ref[0], w_ref[0:half, :]) + _dot(or_ref[0], w_ref[half:, :])
    o_ref[0] = x_ref[0] + gt_ref[0] * acc


def _outproj(o_mla, o_ret, x, gt, w):
    bsz, seq, d = x.shape
    tm = min(OUT_TM, seq)

    def tok(width):
        return pl.BlockSpec((1, tm, width), lambda b, i: (b, i, 0))

    return pl.pallas_call(
        _outproj_kernel,
        out_shape=jax.ShapeDtypeStruct((bsz, seq, d), F32),
        grid=(bsz, seq // tm),
        in_specs=[tok(o_mla.shape[2]), tok(o_ret.shape[2]), tok(d),
                  pl.BlockSpec((1, 1, d), lambda b, i: (b, 0, 0)), _const_spec(w.shape)],
        out_specs=tok(d),
        compiler_params=_params("parallel", "parallel"),
        name="outproj",
    )(o_mla, o_ret, x, gt, w)


def _ffn_kernel(x_ref, xp_ref, xn_ref, sh_ref, sc_ref, gt_ref, g2_ref, gf_ref,
                wup_ref, cw_ref, cb_ref, wdn_ref, o_ref, hs_ref, acc_ref):
    tm = x_ref.shape[1]
    n_inner = wup_ref.shape[0]
    i = pl.program_id(1)
    x = x_ref[0]
    g2 = g2_ref[...]
    scale = 1.0 + sc_ref[0]
    shift = sh_ref[0]
    hs_ref[0:tm, :] = (_rmsnorm(x, g2) * scale + shift).astype(BF16)
    halo = jnp.concatenate([xp_ref[0], xn_ref[0]], axis=0)
    hs_ref[tm:tm + 2 * HALO, :] = (_rmsnorm(halo, g2) * scale + shift).astype(BF16)
    has_prev = (i > 0).astype(F32)
    has_next = (i < pl.num_programs(1) - 1).astype(F32)
    row = lax.broadcasted_iota(jnp.int32, (tm, 2 * FFN_FC), 0)
    acc_ref[...] = jnp.zeros_like(acc_ref)

    def up_proj(j):
        return _dot(hs_ref[...], wup_ref[j])

    def conv_act(u, j):
        um = u[0:tm]
        u_before = u[tm + HALO - 1:tm + HALO] * has_prev
        u_after = u[tm + HALO:tm + HALO + 1] * has_next
        up = jnp.where(row == 0, u_before, pltpu.roll(um, 1, 0))
        un = jnp.where(row == tm - 1, u_after, pltpu.roll(um, tm - 1, 0))
        cw = cw_ref[j]
        cv = up * cw[0:1] + um * cw[1:2] + un * cw[2:3] + cb_ref[j]
        return (_silu(cv[:, FFN_FC:]) * cv[:, :FFN_FC]).astype(BF16)

    u = up_proj(0)
    for j in range(n_inner):
        u_next = up_proj(j + 1) if j + 1 < n_inner else None
        acc_ref[...] += _dot(conv_act(u, j), wdn_ref[j])
        u = u_next
    o_ref[0] =_rmsnorm(x + gt_ref[0] * acc_ref[...], gf_ref[...])


def _ffn(x, sh, sc, gt, g2, gf, wup, cw, cb, wdn):
    bsz, seq, d = x.shape
    tm = min(FFN_TM, seq)
    blocks_per_tile = tm // HALO
    last_block = seq // HALO - 1
    mod_spec = pl.BlockSpec((1, 1, d), lambda b, i: (b, 0, 0))
    return pl.pallas_call(
        _ffn_kernel,
        out_shape=jax.ShapeDtypeStruct((bsz, seq, d), F32),
        grid=(bsz, seq // tm),
        in_specs=[pl.BlockSpec((1, tm, d), lambda b, i: (b, i, 0)),
                  pl.BlockSpec((1, HALO, d), lambda b, i: (b, jnp.maximum(i * blocks_per_tile - 1, 0), 0)),
                  pl.BlockSpec((1, HALO, d), lambda b, i: (b, jnp.minimum((i + 1) * blocks_per_tile, last_block), 0)),
                  mod_spec, mod_spec, mod_spec, _const_spec(g2.shape), _const_spec(gf.shape),
                  _const_spec(wup.shape), _const_spec(cw.shape), _const_spec(cb.shape), _const_spec(wdn.shape)],
        out_specs=pl.BlockSpec((1, tm, d), lambda b, i: (b, i, 0)),
        scratch_shapes=[pltpu.VMEM((tm + 2 * HALO, d), BF16), pltpu.VMEM((tm, d), F32)],
        compiler_params=_params("parallel", "parallel"),
        name="ffn",
    )(x, x, x, sh, sc, gt, g2, gf, wup, cw, cb, wdn)


def _rot_half_cols(w, half):
    shape = w.shape
    w = w.reshape(shape[:-1] + (shape[-1] // (2 * half), 2, half))
    return jnp.stack([-w[..., 1, :], w[..., 0, :]], axis=-2).reshape(shape)


def _rope_angles(pos, dim):
    inv = ROPE_BASE ** (-jnp.arange(0, dim, 2, dtype=F32) / dim)
    return pos.astype(F32)[:, None] * inv[None, :]


def _mla_tables(seq, with_pos):
    if not with_pos:
        return jnp.ones((seq, LANES), F32), jnp.zeros((seq, LANES), F32)
    pos = jnp.arange(seq, dtype=jnp.int32)
    quarter = MLA_ROPE // 2
    ar = _rope_angles(pos // GRID_W, quarter)
    ac = _rope_angles(pos % GRID_W, quarter)
    ang = jnp.concatenate([ar, ar, ac, ac], axis=-1)
    ones = jnp.ones((seq, MLA_NOPE), F32)
    pad = LANES - MLA_NOPE - MLA_ROPE
    cos = jnp.concatenate([ones, jnp.cos(ang), jnp.ones((seq, pad), F32)], axis=-1)
    sin = jnp.concatenate([0.0 * ones, jnp.sin(ang), jnp.zeros((seq, pad), F32)], axis=-1)
    return cos, sin


def _ret_tables(seq, with_pos):
    if not with_pos:
        return jnp.ones((seq, RET_DK), F32), jnp.zeros((seq, RET_DK), F32)
    ang = _rope_angles(jnp.arange(seq, dtype=jnp.int32), RET_DK)
    return (jnp.concatenate([jnp.cos(ang), jnp.cos(ang)], axis=-1),
            jnp.concatenate([-jnp.sin(ang), jnp.sin(ang)], axis=-1))


def _pad_heads(nope, rope):
    r = nope.shape[0]
    pad = jnp.zeros((r, MLA_HEADS, HEAD_PAD - MLA_NOPE - MLA_ROPE), nope.dtype)
    return jnp.concatenate([nope, rope, pad], axis=-1).reshape(r, MLA_HEADS * HEAD_PAD)


def _layout_weights(w_in, w_uq, w_ukv, w_up, conv_w, conv_b, w_down):
    d = w_in.shape[0]
    q_rank = w_uq.shape[0]
    o = np.cumsum([0, q_rank, w_ukv.shape[0], MLA_ROPE, RET_W, RET_W, RET_W, RET_W])
    w_cq, w_ckv, w_kpe, w_rq, w_rk, w_rv, w_rg = [w_in[:, o[t]:o[t + 1]] for t in range(7)]
    rot_half = MLA_ROPE // 4

    def place_rope(w):
        z0 = jnp.zeros((d, MLA_NOPE), w.dtype)
        z1 = jnp.zeros((d, LANES - MLA_NOPE - MLA_ROPE), w.dtype)
        return jnp.concatenate([z0, w, z1], axis=-1)

    win = jnp.concatenate([w_cq, w_ckv, place_rope(w_kpe), place_rope(_rot_half_cols(w_kpe, rot_half)),
                           w_rq, w_rk, w_rv, w_rg], axis=-1).astype(BF16)
    assert win.shape[1] == COLS_IN

    uq = w_uq.reshape(q_rank, MLA_HEADS, MLA_NOPE + MLA_ROPE)
    uq_n, uq_r = uq[..., :MLA_NOPE], uq[..., MLA_NOPE:]
    wuq = jnp.concatenate([_pad_heads(uq_n, uq_r),
                           _pad_heads(jnp.zeros_like(uq_n), _rot_half_cols(uq_r, rot_half))],
                          axis=-1).astype(BF16)

    ukv = w_ukv.reshape(w_ukv.shape[0], MLA_HEADS, MLA_NOPE + MLA_V)
    wk = _pad_heads(ukv[..., :MLA_NOPE], jnp.zeros(ukv.shape[:2] + (MLA_ROPE,), ukv.dtype)).astype(BF16)
    wv = ukv[..., MLA_NOPE:].reshape(w_ukv.shape[0], MLA_HEADS * MLA_V).T.astype(BF16)

    d_ff = w_down.shape[0]
    nj = d_ff // FFN_FC

    def pair_chunks(a):
        lead = a.shape[:-1]
        a = a.reshape(lead + (2, nj, FFN_FC))
        a = jnp.moveaxis(a, -2, 0)
        return a.reshape((nj,) + lead + (2 * FFN_FC,))

    wup = pair_chunks(w_up).astype(BF16)
    cw = pair_chunks(conv_w)
    cb = pair_chunks(conv_b[None, :])
    wdn = w_down.reshape(nj, FFN_FC, w_down.shape[1]).astype(BF16)
    return win, wuq, wk, wv, wup, cw, cb, wdn


def kernel(x, c, ctx, c_ctx, w_ada, b_ada, g_norm1, w_in, g_q, w_uq, g_kv, w_ukv, ret_decay, g_ret,
           w_out, g_norm2, w_up, conv_w, conv_b, w_down, g_final):
    bsz, seq, d = x.shape
    ctx_len = ctx.shape[1]
    depth = w_ada.shape[0]
    assert depth == 1, "single-layer block"
    l = 0

    rows = -(-(bsz + 1) // 8) * 8
    cs = jnp.concatenate([c, c_ctx[None, :], jnp.zeros((rows - bsz - 1, d), F32)], axis=0)
    mod = _adaln(cs, w_ada[l], b_ada[l][None, :])
    sh1, sc1, gt1, sh2, sc2, gt2 = [mod[:bsz, t * d:(t + 1) * d][:, None, :] for t in range(6)]
    shc1 = jnp.broadcast_to(mod[bsz, 0:d][None, None, :], (bsz, 1, d))
    scc1 = jnp.broadcast_to(mod[bsz, d:2 * d][None, None, :], (bsz, 1, d))

    win, wuq, wk, wv, wup, cw, cb, wdn = _layout_weights(w_in[l], w_uq[l], w_ukv[l], w_up[l], conv_w[l],
                                                       conv_b[l], w_down[l])
    g1 = g_norm1[l][None, :]
    gq = g_q[l][None, :]
    gkv = g_kv[l][None, :]

    cm, sm = _mla_tables(seq, True)
    cr, sr = _ret_tables(seq, True)
    q, k, v, rq, rk, rv, sg = _preproj(x, sh1, sc1, g1, win, gq, wuq, gkv, wk, wv, cm, sm, cr, sr,
                                       with_queries=True)
    cm0, sm0 = _mla_tables(ctx_len, False)
    cr0, sr0 = _ret_tables(ctx_len, False)
    k_c, v_c, rk_c, rv_c = _preproj(ctx, shc1, scc1, g1, win, gq, wuq, gkv, wk, wv, cm0, sm0, cr0, sr0,
                                    with_queries=False)

    o_mla = _mla_attn(q, k, k_c, v, v_c)
    rd = jnp.broadcast_to(jnp.transpose(ret_decay[l])[:, :, None], (RET_HEADS, 2, RET_CHUNK)).astype(F32)
    o_ret = _retention(rd, rq, rk, rv, sg, rk_c, rv_c, g_ret[l][None, :])

    x_new = _outproj(o_mla, o_ret, x, gt1, w_out[l].astype(BF16))
    return _ffn(x_new, sh2, sc2, gt2, g_norm2[l][None, :], g_final[None, :], wup, cw, cb, wdn)
```

```python
import functools

import jax
import jax.numpy as jnp
import numpy as np
from jax import lax
from jax.experimental import pallas as pl
from jax.experimental.pallas import tpu as pltpu

F32 = jnp.float32
BF16 = jnp.bfloat16

GRID_W = 64
MLA_HEADS = 8
MLA_NOPE = 64
MLA_ROPE = 32
MLA_V = 64
MLA_SCALE = (MLA_NOPE + MLA_ROPE) ** -0.5
Q_SCALE = MLA_SCALE * float(np.log2(np.e))
SCORE_FLOOR = -1e30
RET_HEADS = 4
RET_DK = 128
RET_DV = 128
ROPE_BASE = 10000.0
EPS = 1e-6

LANES = 128
MXU_DIM = 256
BF16_ROWS = 16
VT_ROWS = MLA_V + BF16_ROWS
VMEM_LIMIT_BYTES = 56 * 1024 * 1024

HEAD_PAD = LANES
PRE_TM = 512
ATT_TQ = 512
ATT_KB = MXU_DIM
ATT_PV_LAG = 4
RET_CHUNK = 256
OUT_TM = 512
FFN_TM = 512
FFN_FC = 256
HALO = 8

NT_DIMS = (((1,), (1,)), ((), ()))
TN_DIMS = (((0,), (0,)), ((), ()))


def _dot(a, b):
    return jnp.dot(a, b, preferred_element_type=F32)


def _rmsnorm(x, g):
    return x * lax.rsqrt(jnp.mean(x * x, axis=-1, keepdims=True) + EPS) * g


def _silu(x):
    return x * (1.0 / (1.0 + jnp.exp(-x)))


def _params(*semantics):
    return pltpu.CompilerParams(dimension_semantics=semantics, vmem_limit_bytes=VMEM_LIMIT_BYTES)


def _const_spec(shape):
    zeros = (0,) * len(shape)
    return pl.BlockSpec(shape, lambda *_: zeros, pipeline_mode=pl.Buffered(1))


def _adaln_kernel(c_ref, w_ref, b_ref, o_ref):
    a = _silu(c_ref[...])
    o_ref[...] = _dot(a.astype(BF16), w_ref[...].astype(BF16)) + b_ref[...]


def _adaln(cs, w, b):
    rows, d = cs.shape
    n = w.shape[1]
    tn = 1024
    return pl.pallas_call(
        _adaln_kernel,
        out_shape=jax.ShapeDtypeStruct((rows, n), F32),
        grid=(n // tn,),
        in_specs=[pl.BlockSpec((rows, d), lambda j: (0, 0)),
                  pl.BlockSpec((d, tn), lambda j: (0, j)),
                  pl.BlockSpec((1, tn), lambda j: (0, j))],
        out_specs=pl.BlockSpec((rows, tn), lambda j: (0, j)),
        compiler_params=_params("parallel"),
        name="adaln",
    )(cs, w, b)


COL_CQ = 0
COL_CKV = 256
COL_KPE = 512
COL_KPE_ROT = 640
COL_RQ = 768
COL_RK = 1280
COL_RV = 1792
COL_RG = 2304
COLS_IN = 2816
RET_W = RET_HEADS * RET_DK


def _preproj_kernel(with_queries, x_ref, sh_ref, sc_ref, g1_ref, win_ref, gq_ref, wuq_ref,
                    gkv_ref, wk_ref, wv_ref, cm_ref, sm_ref, cmt_ref, smt_ref, cr_ref, sr_ref, *out_refs):
    if with_queries:
        q_ref, k_ref, v_ref, rq_ref, rk_ref, rv_ref, sg_ref = out_refs
    else:
        k_ref, v_ref, rk_ref, rv_ref = out_refs
    x = x_ref[0]
    h = _rmsnorm(x, g1_ref[...]) * (1.0 + sc_ref[0]) + sh_ref[0]
    proj = _dot(h.astype(BF16), win_ref[...])
    cm = cm_ref[...]
    sm = sm_ref[...]
    cr = cr_ref[...]
    sr = sr_ref[...]

    ckvn = _rmsnorm(proj[:, COL_CKV:COL_CKV + 256], gkv_ref[...]).astype(BF16)
    kn = _dot(ckvn, wk_ref[...])
    vt = lax.dot_general(wv_ref[...], ckvn, NT_DIMS, preferred_element_type=F32).astype(BF16)
    ones_rows = jnp.ones((BF16_ROWS, vt.shape[1]), BF16)
    for hd in range(MLA_HEADS):
        v_ref[0, hd * VT_ROWS:hd * VT_ROWS + MLA_V, :] = vt[hd * MLA_V:(hd + 1) * MLA_V]
        v_ref[0, hd * VT_ROWS + MLA_V:(hd + 1) * VT_ROWS, :] = ones_rows
    kpe = proj[:, COL_KPE:COL_KPE + LANES] * cm + proj[:, COL_KPE_ROT:COL_KPE_ROT + LANES] * sm
    for hd in range(MLA_HEADS):
        sl = slice(hd * HEAD_PAD, (hd + 1) * HEAD_PAD)
        k_ref[0, :, sl] = (kn[:, sl] + kpe).astype(BF16)

    for hd in range(RET_HEADS):
        sl = slice(hd * RET_DK, (hd + 1) * RET_DK)
        b = proj[:, COL_RK + hd * RET_DK:COL_RK + (hd + 1) * RET_DK]
        rk_ref[0, :, sl] = ((b * cr + pltpu.roll(b, RET_DK // 2, 1) * sr) * (RET_DK ** -0.5)).astype(BF16)
    rv_ref[0] = proj[:, COL_RV:COL_RV + RET_W].astype(BF16)

    if with_queries:
        cqn = _rmsnorm(proj[:, COL_CQ:COL_CQ + 256], gq_ref[...]).astype(BF16)
        qa = lax.dot_general(wuq_ref[...], cqn, NT_DIMS, preferred_element_type=F32)
        cmt = cmt_ref[...]
        smt = smt_ref[...]
        rot0 = MLA_HEADS * HEAD_PAD
        for hd in range(MLA_HEADS):
            sl = slice(hd * HEAD_PAD, (hd + 1) * HEAD_PAD)
            a = qa[sl, :]
            b = qa[rot0 + hd * HEAD_PAD:rot0 + (hd + 1) * HEAD_PAD, :]
            q_ref[0, sl, :] = ((a * cmt + b * smt) * Q_SCALE).astype(BF16)
        for hd in range(RET_HEADS):
            sl = slice(hd * RET_DK, (hd + 1) * RET_DK)
            a = proj[:, COL_RQ + hd * RET_DK:COL_RQ + (hd + 1) * RET_DK]
            rq_ref[0, :, sl] = (a * cr + pltpu.roll(a, RET_DK // 2, 1) * sr).astype(BF16)
        sg_ref[0] = _silu(proj[:, COL_RG:COL_RG + RET_W]).astype(BF16)


def _preproj(x, sh, sc, g1, win, gq, wuq, gkv, wk, wv, cm, sm, cr, sr, *, with_queries):
    bsz, seq, d = x.shape
    tm = min(PRE_TM, seq)
    qk_w = MLA_HEADS * HEAD_PAD
    v_w = MLA_HEADS * VT_ROWS
    cmt, smt = cm.T, sm.T
    tabt_spec = pl.BlockSpec((LANES, tm), lambda b, i: (0, i))
    qt_shape = jax.ShapeDtypeStruct((bsz, qk_w, seq), BF16)
    qt_spec = pl.BlockSpec((1, qk_w, tm), lambda b, i: (b, 0, i))

    def tok(width):
        return pl.BlockSpec((1, tm, width), lambda b, i: (b, i, 0))

    def out(width):
        return jax.ShapeDtypeStruct((bsz, seq, width), BF16)

    mod_spec = pl.BlockSpec((1, 1, d), lambda b, i: (b, 0, 0))
    tab_spec = pl.BlockSpec((tm, LANES), lambda b, i: (i, 0))
    vt_shape = jax.ShapeDtypeStruct((bsz, v_w, seq), BF16)
    vt_spec = pl.BlockSpec((1, v_w, tm), lambda b, i: (b, 0, i))
    if with_queries:
        out_shape = [qt_shape, out(qk_w), vt_shape, out(RET_W), out(RET_W), out(RET_W), out(RET_W)]
        out_specs = [qt_spec, tok(qk_w), vt_spec, tok(RET_W), tok(RET_W), tok(RET_W), tok(RET_W)]
    else:
        out_shape = [out(qk_w), vt_shape, out(RET_W), out(RET_W)]
        out_specs = [tok(qk_w), vt_spec, tok(RET_W), tok(RET_W)]
    return pl.pallas_call(
        functools.partial(_preproj_kernel, with_queries),
        out_shape=out_shape,
        grid=(bsz, seq // tm),
        in_specs=[tok(d), mod_spec, mod_spec, _const_spec(g1.shape), _const_spec(win.shape),
                  _const_spec(gq.shape), _const_spec(wuq.shape), _const_spec(gkv.shape),
                  _const_spec(wk.shape), _const_spec(wv.shape), tab_spec, tab_spec, tabt_spec, tabt_spec,
                  tab_spec, tab_spec],
        out_specs=out_specs,
        compiler_params=_params("parallel", "parallel"),
        name="preproj_q" if with_queries else "preproj_ctx",
    )(x, sh, sc, g1, win, gq, wuq, gkv, wk, wv, cm, sm, cmt, smt, cr, sr)


def _mla_attn_kernel(q_ref, kl_ref, kc_ref, vl_ref, vc_ref, o_ref, s_ref):
    kb = ATT_KB
    n_lat = kl_ref.shape[1] // kb
    n_blk = n_lat + kc_ref.shape[1] // kb
    pair_rows = 2 * MLA_V

    def k_block(h, b):
        hs = slice(h * HEAD_PAD, (h + 1) * HEAD_PAD)
        if b < n_lat:
            return kl_ref[0, b * kb:(b + 1) * kb, hs]
        return kc_ref[0, (b - n_lat) * kb:(b - n_lat + 1) * kb, hs]

    def vt_block(h, b):
        rs = slice(h * VT_ROWS, (h + 1) * VT_ROWS)
        if b < n_lat:
            return vl_ref[0, rs, b * kb:(b + 1) * kb]
        return vc_ref[0, rs, (b - n_lat) * kb:(b - n_lat + 1) * kb]

    n_units = MLA_HEADS * n_blk
    exp_lag = n_blk
    pv_lag = exp_lag + ATT_PV_LAG
    m_run = [None] * MLA_HEADS
    ot = [None] * MLA_HEADS
    p_blocks = {}
    for t in range(n_units + pv_lag):
        bm = None
        if t < n_units:
            h, b = divmod(t, n_blk)
            qt = q_ref[0, h * HEAD_PAD:(h + 1) * HEAD_PAD, :]
            s = _dot(k_block(h, b), qt)
            s_ref[h % 2, b * kb:(b + 1) * kb, :] = s
            bm = jnp.max(s, axis=0, keepdims=True)
            m_run[h] = bm if m_run[h] is None else jnp.maximum(m_run[h], bm)
        e = t - exp_lag
        if 0 <= e < n_units:
            h, b = divmod(e, n_blk)
            m_use = m_run[h]
            if bm is not None:
                m_use = jnp.maximum(m_use, jnp.minimum(bm, SCORE_FLOOR))
            p_blocks[e] = jnp.exp2(s_ref[h % 2, b * kb:(b + 1) * kb, :] - m_use).astype(BF16)
        v = t - pv_lag
        if 0 <= v < n_units:
            h, b = divmod(v, n_blk)
            pv = _dot(vt_block(h, b), p_blocks.pop(v))
            ot[h] = pv if ot[h] is None else ot[h] + pv
            if b == n_blk - 1:
                o_h = ot[h][:MLA_V] * (1.0 / ot[h][MLA_V:MLA_V + 1])
                ot[h] = o_h
                if h % 2 == 1:
                    pair_t = jnp.concatenate([ot[h - 1], o_h], axis=0)
                    cols = slice((h // 2) * pair_rows, (h // 2 + 1) * pair_rows)
                    o_ref[0, :, cols] = pair_t.T.astype(BF16)


def _mla_attn(qt, k_lat, k_ctx, vt_lat, vt_ctx):
    bsz, qk_w, seq = qt.shape
    ctx_len = k_ctx.shape[1]
    v_w = vt_lat.shape[1]
    o_w = MLA_HEADS * MLA_V
    tq = min(ATT_TQ, seq)
    assert seq % ATT_KB == 0 and ctx_len % ATT_KB == 0
    return pl.pallas_call(
        _mla_attn_kernel,
        out_shape=jax.ShapeDtypeStruct((bsz, seq, o_w), BF16),
        grid=(bsz, seq // tq),
        in_specs=[pl.BlockSpec((1, qk_w, tq), lambda b, i: (b, 0, i)),
                  pl.BlockSpec((1, seq, qk_w), lambda b, i: (b, 0, 0)),
                  pl.BlockSpec((1, ctx_len, qk_w), lambda b, i: (b, 0, 0)),
                  pl.BlockSpec((1, v_w, seq), lambda b, i: (b, 0, 0)),
                  pl.BlockSpec((1, v_w, ctx_len), lambda b, i: (b, 0, 0))],
        out_specs=pl.BlockSpec((1, tq, o_w), lambda b, i: (b, i, 0)),
        scratch_shapes=[pltpu.VMEM((2, seq + ctx_len, tq), F32)],
        compiler_params=_params("parallel", "parallel"),
        name="mla_attn",
    )(qt, k_lat, k_ctx, vt_lat, vt_ctx)


def _retention_kernel(rd_ref, q_ref, k_ref, v_ref, sg_ref, kc_ref, vc_ref, g_ref, o_ref, pf_ref, sb_ref):
    c = RET_CHUNK
    n_chunks = q_ref.shape[1] // c
    lgf = -jnp.exp(rd_ref[0, 0:1, :])
    lgb = -jnp.exp(rd_ref[0, 1:2, :])
    lgf1 = lgf[:, :RET_DV]
    lgb1 = lgb[:, :RET_DV]
    row = lax.broadcasted_iota(jnp.int32, (c, RET_DV), 0).astype(F32)
    zeta_f = jnp.exp(lgf1 * (c - 1.0 - row))
    zeta_b = jnp.exp(lgb1 * row)
    xi_f = jnp.exp(lgf1 * (row + 1.0))
    xi_b = jnp.exp(lgb1 * (c - row))
    gc_f = jnp.exp(lgf1 * float(c))
    gc_b = jnp.exp(lgb1 * float(c))
    diff = (lax.broadcasted_iota(jnp.int32, (c, c), 0) - lax.broadcasted_iota(jnp.int32, (c, c), 1)).astype(F32)
    dmat = jnp.where(diff >= 0.0, jnp.exp(lgf * jnp.maximum(diff, 0.0)), jnp.exp(lgb * jnp.maximum(-diff, 0.0)))

    def kv_outer(kk, vv):
        vf = vv.astype(F32)
        v2 = jnp.concatenate([vf * zeta_f, vf * zeta_b], axis=1).astype(BF16)
        return lax.dot_general(kk, v2, TN_DIMS, preferred_element_type=F32)

    p_ctx = kv_outer(kc_ref[0], vc_ref[0])
    sf = p_ctx[:, :RET_DV]
    sb = p_ctx[:, RET_DV:]

    for ci in reversed(range(n_chunks)):
        rows = slice(ci * c, (ci + 1) * c)
        p = kv_outer(k_ref[0, rows, :], v_ref[0, rows, :])
        pf_ref[ci] = p[:, :RET_DV]
        sb_ref[ci] = sb
        sb = gc_b * sb + p[:, RET_DV:]

    g = g_ref[...]
    for ci in range(n_chunks):
        rows = slice(ci * c, (ci + 1) * c)
        qc = q_ref[0, rows, :]
        s = lax.dot_general(qc, k_ref[0, rows, :], NT_DIMS, preferred_element_type=F32)
        inner = _dot((s * dmat).astype(BF16), v_ref[0, rows, :])
        st = jnp.concatenate([sf, sb_ref[ci]], axis=1).astype(BF16)
        cross = _dot(qc, st)
        y = inner + cross[:, :RET_DV] * xi_f + cross[:, RET_DV:] * xi_b
        mu = jnp.mean(y, axis=-1, keepdims=True)
        d = y - mu
        var = jnp.mean(d * d, axis=-1, keepdims=True)
        yn = d * lax.rsqrt(var + EPS) * g
        o_ref[0, rows, :] = (sg_ref[0, rows, :].astype(F32) * yn).astype(BF16)
        sf = gc_f * sf + pf_ref[ci]


def _retention(rd, rq, rk, rv, sg, rk_ctx, rv_ctx, g_ret):
    bsz, seq, _ = rq.shape
    ctx_len = rk_ctx.shape[1]
    assert ctx_len == RET_CHUNK and seq % RET_CHUNK == 0

    def head(rows):
        return pl.BlockSpec((1, rows, RET_DK), lambda b, h: (b, 0, h))

    return pl.pallas_call(
        _retention_kernel,
        out_shape=jax.ShapeDtypeStruct((bsz, seq, RET_HEADS * RET_DV), BF16),
        grid=(bsz, RET_HEADS),
        in_specs=[pl.BlockSpec((1, 2, RET_CHUNK), lambda b, h: (h, 0, 0)),
                  head(seq), head(seq), head(seq), head(seq), head(ctx_len), head(ctx_len),
                  pl.BlockSpec((1, RET_DV), lambda b, h: (0, h))],
        out_specs=head(seq),
        scratch_shapes=[pltpu.VMEM((seq // RET_CHUNK, RET_DK, RET_DV), F32),
                        pltpu.VMEM((seq // RET_CHUNK, RET_DK, RET_DV), F32)],
        compiler_params=_params("parallel", "parallel"),
        name="retention",
    )(rd, rq, rk, rv, sg, rk_ctx, rv_ctx, g_ret)


def _outproj_kernel(om_ref, or_ref, x_ref, gt_ref, w_ref, o_ref):
    half = om_ref.shape[2]
    acc = _dot(om_ref[0], w_ref[0:half, :]) + _dot(or_ref[0], w_ref[half:, :])
    o_ref[0] = x_ref[0] + gt_ref[0] * acc


def _outproj(o_mla, o_ret, x, gt, w):
    bsz, seq, d = x.shape
    tm = min(OUT_TM, seq)

    def tok(width):
        return pl.BlockSpec((1, tm, width), lambda b, i: (b, i, 0))

    return pl.pallas_call(
        _outproj_kernel,
        out_shape=jax.ShapeDtypeStruct((bsz, seq, d), F32),
        grid=(bsz, seq // tm),
        in_specs=[tok(o_mla.shape[2]), tok(o_ret.shape[2]), tok(d),
                  pl.BlockSpec((1, 1, d), lambda b, i: (b, 0, 0)), _const_spec(w.shape)],
        out_specs=tok(d),
        compiler_params=_params("parallel", "parallel"),
        name="outproj",
    )(o_mla, o_ret, x, gt, w)


def _ffn_kernel(x_ref, xp_ref, xn_ref, sh_ref, sc_ref, gt_ref, g2_ref, gf_ref,
                wup_ref, cw_ref, cb_ref, wdn_ref, o_ref, hs_ref, acc_ref):
    tm = x_ref.shape[1]
    n_inner = wup_ref.shape[0]
    i = pl.program_id(1)
    x = x_ref[0]
    g2 = g2_ref[...]
    scale = 1.0 + sc_ref[0]
    shift = sh_ref[0]
    hs_ref[0:tm, :] = (_rmsnorm(x, g2) * scale + shift).astype(BF16)
    halo = jnp.concatenate([xp_ref[0], xn_ref[0]], axis=0)
    hs_ref[tm:tm + 2 * HALO, :] = (_rmsnorm(halo, g2) * scale + shift).astype(BF16)
    has_prev = (i > 0).astype(F32)
    has_next = (i < pl.num_programs(1) - 1).astype(F32)
    row = lax.broadcasted_iota(jnp.int32, (tm, 2 * FFN_FC), 0)
    acc_ref[...] = jnp.zeros_like(acc_ref)

    def up_proj(j):
        return _dot(hs_ref[...], wup_ref[j])

    def conv_act(u, j):
        um = u[0:tm]
        u_before = u[tm + HALO - 1:tm + HALO] * has_prev
        u_after = u[tm + HALO:tm + HALO + 1] * has_next
        up = jnp.where(row == 0, u_before, pltpu.roll(um, 1, 0))
        un = jnp.where(row == tm - 1, u_after, pltpu.roll(um, tm - 1, 0))
        cw = cw_ref[j]
        cv = up * cw[0:1] + um * cw[1:2] + un * cw[2:3] + cb_ref[j]
        return (_silu(cv[:, FFN_FC:]) * cv[:, :FFN_FC]).astype(BF16)

    u = up_proj(0)
    for j in range(n_inner):
        u_next = up_proj(j + 1) if j + 1 < n_inner else None
        acc_ref[...] += _dot(conv_act(u, j), wdn_ref[j])
        u = u_next
    o_ref[0] =_rmsnorm(x + gt_ref[0] * acc_ref[...], gf_ref[...])


def _ffn(x, sh, sc, gt, g2, gf, wup, cw, cb, wdn):
    bsz, seq, d = x.shape
    tm = min(FFN_TM, seq)
    blocks_per_tile = tm // HALO
    last_block = seq // HALO - 1
    mod_spec = pl.BlockSpec((1, 1, d), lambda b, i: (b, 0, 0))
    return pl.pallas_call(
        _ffn_kernel,
        out_shape=jax.ShapeDtypeStruct((bsz, seq, d), F32),
        grid=(bsz, seq // tm),
        in_specs=[pl.BlockSpec((1, tm, d), lambda b, i: (b, i, 0)),
                  pl.BlockSpec((1, HALO, d), lambda b, i: (b, jnp.maximum(i * blocks_per_tile - 1, 0), 0)),
                  pl.BlockSpec((1, HALO, d), lambda b, i: (b, jnp.minimum((i + 1) * blocks_per_tile, last_block), 0)),
                  mod_spec, mod_spec, mod_spec, _const_spec(g2.shape), _const_spec(gf.shape),
                  _const_spec(wup.shape), _const_spec(cw.shape), _const_spec(cb.shape), _const_spec(wdn.shape)],
        out_specs=pl.BlockSpec((1, tm, d), lambda b, i: (b, i, 0)),
        scratch_shapes=[pltpu.VMEM((tm + 2 * HALO, d), BF16), pltpu.VMEM((tm, d), F32)],
        compiler_params=_params("parallel", "parallel"),
        name="ffn",
    )(x, x, x, sh, sc, gt, g2, gf, wup, cw, cb, wdn)


def _rot_half_cols(w, half):
    shape = w.shape
    w = w.reshape(shape[:-1] + (shape[-1] // (2 * half), 2, half))
    return jnp.stack([-w[..., 1, :], w[..., 0, :]], axis=-2).reshape(shape)


def _rope_angles(pos, dim):
    inv = ROPE_BASE ** (-np.arange(0, dim, 2, dtype=np.float64) / dim)
    return pos.astype(np.float64)[:, None] * inv[None, :]


def _mla_tables(seq, with_pos):
    if not with_pos:
        return np.ones((seq, LANES), np.float32), np.zeros((seq, LANES), np.float32)
    pos = np.arange(seq)
    quarter = MLA_ROPE // 2
    ar = _rope_angles(pos // GRID_W, quarter)
    ac = _rope_angles(pos % GRID_W, quarter)
    ang = np.concatenate([ar, ar, ac, ac], axis=-1)
    pad = LANES - MLA_NOPE - MLA_ROPE
    cos = np.concatenate([np.ones((seq, MLA_NOPE)), np.cos(ang), np.ones((seq, pad))], axis=-1)
    sin = np.concatenate([np.zeros((seq, MLA_NOPE)), np.sin(ang), np.zeros((seq, pad))], axis=-1)
    return cos.astype(np.float32), sin.astype(np.float32)


def _ret_tables(seq, with_pos):
    if not with_pos:
        return np.ones((seq, RET_DK), np.float32), np.zeros((seq, RET_DK), np.float32)
    ang = _rope_angles(np.arange(seq), RET_DK)
    return (np.concatenate([np.cos(ang), np.cos(ang)], axis=-1).astype(np.float32),
            np.concatenate([-np.sin(ang), np.sin(ang)], axis=-1).astype(np.float32))


def _pad_heads(nope, rope):
    r = nope.shape[0]
    pad = jnp.zeros((r, MLA_HEADS, HEAD_PAD - MLA_NOPE - MLA_ROPE), nope.dtype)
    return jnp.concatenate([nope, rope, pad], axis=-1).reshape(r, MLA_HEADS * HEAD_PAD)


def _pair_up_kernel(a_ref, g_ref, o_ref):
    o_ref[0, :, :FFN_FC] = a_ref[...].astype(BF16)
    o_ref[0, :, FFN_FC:] = g_ref[...].astype(BF16)


def _pair_up_weights(w_up, nj):
    d = w_up.shape[0]
    return pl.pallas_call(
        _pair_up_kernel,
        out_shape=jax.ShapeDtypeStruct((nj, d, 2 * FFN_FC), BF16),
        grid=(nj,),
        in_specs=[pl.BlockSpec((d, FFN_FC), lambda j: (0, j)),
                  pl.BlockSpec((d, FFN_FC), lambda j: (0, nj + j))],
        out_specs=pl.BlockSpec((1, d, 2 * FFN_FC), lambda j: (j, 0, 0)),
        compiler_params=_params("parallel"),
        name="pair_up_weights",
    )(w_up, w_up)


def _layout_weights(w_in, w_uq, w_ukv, w_up, conv_w, conv_b, w_down):
    d = w_in.shape[0]
    q_rank = w_uq.shape[0]
    o = np.cumsum([0, q_rank, w_ukv.shape[0], MLA_ROPE, RET_W, RET_W, RET_W, RET_W])
    w_cq, w_ckv, w_kpe, w_rq, w_rk, w_rv, w_rg = [w_in[:, o[t]:o[t + 1]] for t in range(7)]
    rot_half = MLA_ROPE // 4

    def place_rope(w):
        z0 = jnp.zeros((d, MLA_NOPE), w.dtype)
        z1 = jnp.zeros((d, LANES - MLA_NOPE - MLA_ROPE), w.dtype)
        return jnp.concatenate([z0, w, z1], axis=-1)

    win = jnp.concatenate([w_cq, w_ckv, place_rope(w_kpe), place_rope(_rot_half_cols(w_kpe, rot_half)),
                           w_rq, w_rk, w_rv, w_rg], axis=-1).astype(BF16)
    assert win.shape[1] == COLS_IN

    uq = w_uq.reshape(q_rank, MLA_HEADS, MLA_NOPE + MLA_ROPE)
    uq_n, uq_r = uq[..., :MLA_NOPE], uq[..., MLA_NOPE:]
    wuq = jnp.concatenate([_pad_heads(uq_n, uq_r),
                           _pad_heads(jnp.zeros_like(uq_n), _rot_half_cols(uq_r, rot_half))],
                          axis=-1).T.astype(BF16)

    ukv =w_ukv.reshape(w_ukv.shape[0], MLA_HEADS, MLA_NOPE + MLA_V)
    wk = _pad_heads(ukv[..., :MLA_NOPE], jnp.zeros(ukv.shape[:2] + (MLA_ROPE,), ukv.dtype)).astype(BF16)
    wv = ukv[..., MLA_NOPE:].reshape(w_ukv.shape[0], MLA_HEADS * MLA_V).T.astype(BF16)

    d_ff = w_down.shape[0]
    nj = d_ff // FFN_FC

    def pair_chunks(a):
        lead = a.shape[:-1]
        a = a.reshape(lead + (2, nj, FFN_FC))
        a = jnp.moveaxis(a, -2, 0)
        return a.reshape((nj,) + lead + (2 * FFN_FC,))

    wup = _pair_up_weights(w_up, nj)
    cw = pair_chunks(conv_w)
    cb = pair_chunks(conv_b[None, :])
    wdn = w_down.reshape(nj, FFN_FC, w_down.shape[1]).astype(BF16)
    return win, wuq, wk, wv, wup, cw, cb, wdn


def kernel(x, c, ctx, c_ctx, w_ada, b_ada, g_norm1, w_in, g_q, w_uq, g_kv, w_ukv, ret_decay, g_ret,
           w_out, g_norm2, w_up, conv_w, conv_b, w_down, g_final):
    bsz, seq, d = x.shape
    ctx_len = ctx.shape[1]
    depth = w_ada.shape[0]
    assert depth == 1, "single-layer block"
    l = 0

    rows = -(-(bsz + 1) // 8) * 8
    cs = jnp.concatenate([c, c_ctx[None, :], jnp.zeros((rows - bsz - 1, d), F32)], axis=0)
    mod = _adaln(cs, w_ada[l], b_ada[l][None, :])
    sh1, sc1, gt1, sh2, sc2, gt2 = [mod[:bsz, t * d:(t + 1) * d][:, None, :] for t in range(6)]
    shc1 = jnp.broadcast_to(mod[bsz, 0:d][None, None, :], (bsz, 1, d))
    scc1 = jnp.broadcast_to(mod[bsz, d:2 * d][None, None, :], (bsz, 1, d))

    win, wuq, wk, wv, wup, cw, cb, wdn = _layout_weights(w_in[l], w_uq[l], w_ukv[l], w_up[l], conv_w[l],
                                                       conv_b[l], w_down[l])
    g1 = g_norm1[l][None, :]
    gq = g_q[l][None, :]
    gkv = g_kv[l][None, :]

    cm, sm = _mla_tables(seq, True)
    cr, sr = _ret_tables(seq, True)
    q, k, v, rq, rk, rv, sg = _preproj(x, sh1, sc1, g1, win, gq, wuq, gkv, wk, wv, cm, sm, cr, sr,
                                       with_queries=True)
    cm0, sm0 = _mla_tables(ctx_len, False)
    cr0, sr0 = _ret_tables(ctx_len, False)
    k_c, v_c, rk_c, rv_c = _preproj(ctx, shc1, scc1, g1, win, gq, wuq, gkv, wk, wv, cm0, sm0, cr0, sr0,
                                    with_queries=False)

    o_mla = _mla_attn(q, k, k_c, v, v_c)
    rd = jnp.broadcast_to(jnp.transpose(ret_decay[l])[:, :, None], (RET_HEADS, 2, RET_CHUNK)).astype(F32)
    o_ret = _retention(rd, rq, rk, rv, sg, rk_c, rv_c, g_ret[l][None, :])

    x_new = _outproj(o_mla, o_ret, x, gt1, w_out[l].astype(BF16))
    return _ffn(x_new, sh2, sc2, gt2, g_norm2[l][None, :], g_final[None, :], wup, cw, cb, wdn)
```

```python
import functools

import jax
import jax.numpy as jnp
import numpy as np
from jax import lax
from jax.experimental import pallas as pl
from jax.experimental.pallas import tpu as pltpu

F32 = jnp.float32
BF16 = jnp.bfloat16

GRID_W = 64
MLA_HEADS = 8
MLA_NOPE = 64
MLA_ROPE = 32
MLA_V = 64
MLA_SCALE = (MLA_NOPE + MLA_ROPE) ** -0.5
Q_SCALE = MLA_SCALE * float(np.log2(np.e))
SCORE_FLOOR = -1e30
RET_HEADS = 4
RET_DK = 128
RET_DV = 128
ROPE_BASE = 10000.0
EPS = 1e-6

LANES = 128
MXU_DIM = 256
BF16_ROWS = 16
VT_ROWS = MLA_V + BF16_ROWS
VMEM_LIMIT_BYTES = 56 * 1024 * 1024

HEAD_PAD = LANES
PRE_TM = 512
ATT_TQ = 512
ATT_SUBTILES = 2
ATT_KB = MXU_DIM
ATT_PV_LAG = 4
RET_CHUNK = 256
FFN_TM = 1024
FFN_FC = 256
HALO = 8

NT_DIMS = (((1,), (1,)), ((), ()))
TN_DIMS = (((0,), (0,)), ((), ()))


def _dot(a, b):
    return jnp.dot(a, b, preferred_element_type=F32)


def _rmsnorm(x, g):
    return x * lax.rsqrt(jnp.mean(x * x, axis=-1, keepdims=True) + EPS) * g


def _silu(x):
    return x * (1.0 / (1.0 + jnp.exp(-x)))


def _params(*semantics):
    return pltpu.CompilerParams(dimension_semantics=semantics, vmem_limit_bytes=VMEM_LIMIT_BYTES)


def _const_spec(shape):
    zeros = (0,) * len(shape)
    return pl.BlockSpec(shape, lambda *_: zeros, pipeline_mode=pl.Buffered(1))


def _adaln_kernel(c_ref, w_ref, b_ref, o_ref):
    a = _silu(c_ref[...])
    o_ref[...] = _dot(a.astype(BF16), w_ref[0].astype(BF16)) + b_ref[...]


def _adaln(cs, w, b, layer):
    rows, d = cs.shape
    n = w.shape[2]
    tn = 1024
    return pl.pallas_call(
        _adaln_kernel,
        out_shape=jax.ShapeDtypeStruct((rows, n), F32),
        grid=(n // tn,),
        in_specs=[pl.BlockSpec((rows, d), lambda j: (0, 0)),
                  pl.BlockSpec((1, d, tn), lambda j: (layer, 0, j)),
                  pl.BlockSpec((1, tn), lambda j: (0, j))],
        out_specs=pl.BlockSpec((rows, tn), lambda j: (0, j)),
        compiler_params=_params("parallel"),
        name="adaln",
    )(cs, w, b)


COL_CQ = 0
COL_CKV = 256
COL_KPE = 512
COL_KPE_ROT = 640
COL_RQ = 768
COL_RK = 1280
COL_RV = 1792
COL_RG = 2304
COLS_IN = 2816
RET_W = RET_HEADS * RET_DK


def _preproj_kernel(with_queries, x_ref, sh_ref, sc_ref, g1_ref, win_ref, gq_ref, wuq_ref,
                    gkv_ref, wk_ref, wv_ref, cm_ref, sm_ref, cmt_ref, smt_ref, cr_ref, sr_ref, *out_refs):
    if with_queries:
        q_ref, k_ref, v_ref, rq_ref, rk_ref, rv_ref, sg_ref = out_refs
    else:
        k_ref, v_ref, rk_ref, rv_ref = out_refs
    x = x_ref[0]
    h = _rmsnorm(x, g1_ref[...] * (1.0 + sc_ref[0])) + sh_ref[0]
    proj = _dot(h.astype(BF16), win_ref[...])
    cm = cm_ref[...]
    sm = sm_ref[...]
    cr = cr_ref[...]
    sr = sr_ref[...]

    ckvn = _rmsnorm(proj[:, COL_CKV:COL_CKV + 256], gkv_ref[...]).astype(BF16)
    kn = _dot(ckvn, wk_ref[...])
    vt = lax.dot_general(wv_ref[...], ckvn, NT_DIMS, preferred_element_type=F32).astype(BF16)
    ones_rows = jnp.ones((BF16_ROWS, vt.shape[1]), BF16)
    for hd in range(MLA_HEADS):
        v_ref[0, hd * VT_ROWS:hd * VT_ROWS + MLA_V, :] = vt[hd * MLA_V:(hd + 1) * MLA_V]
        v_ref[0, hd * VT_ROWS + MLA_V:(hd + 1) * VT_ROWS, :] = ones_rows
    kpe = proj[:, COL_KPE:COL_KPE + LANES] * cm + proj[:, COL_KPE_ROT:COL_KPE_ROT + LANES] * sm
    for hd in range(MLA_HEADS):
        sl = slice(hd * HEAD_PAD, (hd + 1) * HEAD_PAD)
        k_ref[0, :, sl] = (kn[:, sl] + kpe).astype(BF16)

    for hd in range(RET_HEADS):
        sl = slice(hd * RET_DK, (hd + 1) * RET_DK)
        b = proj[:, COL_RK + hd * RET_DK:COL_RK + (hd + 1) * RET_DK]
        rk_ref[0, :, sl] = ((b * cr + pltpu.roll(b, RET_DK // 2, 1) * sr) * (RET_DK ** -0.5)).astype(BF16)
    rv_ref[0] = proj[:, COL_RV:COL_RV + RET_W].astype(BF16)

    if with_queries:
        cqn = _rmsnorm(proj[:, COL_CQ:COL_CQ + 256], gq_ref[...]).astype(BF16)
        qa = lax.dot_general(wuq_ref[...], cqn, NT_DIMS, preferred_element_type=F32)
        cmt = cmt_ref[...]
        smt = smt_ref[...]
        rot0 = MLA_HEADS * HEAD_PAD
        for hd in range(MLA_HEADS):
            sl = slice(hd * HEAD_PAD, (hd + 1) * HEAD_PAD)
            a = qa[sl, :]
            b = qa[rot0 + hd * HEAD_PAD:rot0 + (hd + 1) * HEAD_PAD, :]
            q_ref[0, sl, :] = ((a * cmt + b * smt) * Q_SCALE).astype(BF16)
        for hd in range(RET_HEADS):
            sl = slice(hd * RET_DK, (hd + 1) * RET_DK)
            a = proj[:, COL_RQ + hd * RET_DK:COL_RQ + (hd + 1) * RET_DK]
            rq_ref[0, :, sl] = (a * cr + pltpu.roll(a, RET_DK // 2, 1) * sr).astype(BF16)
        sg_ref[0] = _silu(proj[:, COL_RG:COL_RG + RET_W]).astype(BF16)


def _preproj(x, sh, sc, g1, win, gq, wuq, gkv, wk, wv, cm, sm, cr, sr, *, with_queries):
    bsz, seq, d = x.shape
    tm = min(PRE_TM, seq)
    qk_w = MLA_HEADS * HEAD_PAD
    v_w = MLA_HEADS * VT_ROWS
    cmt, smt = cm.T, sm.T
    tabt_spec = pl.BlockSpec((LANES, tm), lambda b, i: (0, i))
    qt_shape = jax.ShapeDtypeStruct((bsz, qk_w, seq), BF16)
    qt_spec = pl.BlockSpec((1, qk_w, tm), lambda b, i: (b, 0, i))

    def tok(width):
        return pl.BlockSpec((1, tm, width), lambda b, i: (b, i, 0))

    def out(width):
        return jax.ShapeDtypeStruct((bsz, seq, width), BF16)

    mod_spec = pl.BlockSpec((1, 1, d), lambda b, i: (b, 0, 0))
    tab_spec = pl.BlockSpec((tm, LANES), lambda b, i: (i, 0))
    vt_shape = jax.ShapeDtypeStruct((bsz, v_w, seq), BF16)
    vt_spec = pl.BlockSpec((1, v_w, tm), lambda b, i: (b, 0, i))
    if with_queries:
        out_shape = [qt_shape, out(qk_w), vt_shape, out(RET_W), out(RET_W), out(RET_W), out(RET_W)]
        out_specs = [qt_spec, tok(qk_w), vt_spec, tok(RET_W), tok(RET_W), tok(RET_W), tok(RET_W)]
    else:
        out_shape = [out(qk_w), vt_shape, out(RET_W), out(RET_W)]
        out_specs = [tok(qk_w), vt_spec, tok(RET_W), tok(RET_W)]
    return pl.pallas_call(
        functools.partial(_preproj_kernel, with_queries),
        out_shape=out_shape,
        grid=(bsz, seq // tm),
        in_specs=[tok(d), mod_spec, mod_spec, _const_spec(g1.shape), _const_spec(win.shape),
                  _const_spec(gq.shape), _const_spec(wuq.shape), _const_spec(gkv.shape),
                  _const_spec(wk.shape), _const_spec(wv.shape), tab_spec, tab_spec, tabt_spec, tabt_spec,
                  tab_spec, tab_spec],
        out_specs=out_specs,
        compiler_params=_params("parallel", "parallel"),
        name="preproj_q" if with_queries else "preproj_ctx",
    )(x, sh, sc, g1, win, gq, wuq, gkv, wk, wv, cm, sm, cmt, smt, cr, sr)


def _mla_attn_kernel(q_ref, kl_ref, kc_ref, vl_ref, vc_ref, o_ref, s_ref):
    kb = ATT_KB
    n_lat = kl_ref.shape[1] // kb
    n_blk = n_lat + kc_ref.shape[1] // kb
    pair_rows = 2 * MLA_V

    def k_block(h, b):
        hs = slice(h * HEAD_PAD, (h + 1) * HEAD_PAD)
        if b < n_lat:
            return kl_ref[0, b * kb:(b + 1) * kb, hs]
        return kc_ref[0, (b - n_lat) * kb:(b - n_lat + 1) * kb, hs]

    def vt_block(h, b):
        rs = slice(h * VT_ROWS, (h + 1) * VT_ROWS)
        if b < n_lat:
            return vl_ref[0, rs, b * kb:(b + 1) * kb]
        return vc_ref[0, rs, (b - n_lat) * kb:(b - n_lat + 1) * kb]

    tq = s_ref.shape[2]
    n_rows = (q_ref.shape[2] // tq) * MLA_HEADS
    n_units = n_rows * n_blk
    exp_lag = n_blk
    pv_lag = exp_lag + ATT_PV_LAG
    m_run = [None] * n_rows
    ot = [None] * n_rows
    p_blocks = {}
    for t in range(n_units + pv_lag):
        bm = None
        if t < n_units:
            r, b = divmod(t, n_blk)
            qi, h = divmod(r, MLA_HEADS)
            qt = q_ref[0, h * HEAD_PAD:(h + 1) * HEAD_PAD, qi * tq:(qi + 1) * tq]
            s = _dot(k_block(h, b), qt)
            s_ref[r % 2, b * kb:(b + 1) * kb, :] = s
            bm = jnp.max(s, axis=0, keepdims=True)
            m_run[r] = bm if m_run[r] is None else jnp.maximum(m_run[r], bm)
        e = t - exp_lag
        if 0 <= e < n_units:
            r, b = divmod(e, n_blk)
            m_use = m_run[r]
            if bm is not None:
                m_use = jnp.maximum(m_use, jnp.minimum(bm, SCORE_FLOOR))
            p_blocks[e] = jnp.exp2(s_ref[r % 2, b * kb:(b + 1) * kb, :] - m_use).astype(BF16)
        v = t - pv_lag
        if 0 <= v < n_units:
            r, b = divmod(v, n_blk)
            qi, h = divmod(r, MLA_HEADS)
            pv = _dot(vt_block(h, b), p_blocks.pop(v))
            ot[r] = pv if ot[r] is None else ot[r] + pv
            if b == n_blk - 1:
                o_h = ot[r][:MLA_V] * (1.0 / ot[r][MLA_V:MLA_V + 1])
                ot[r] = o_h
                if h % 2 == 1:
                    pair_t = jnp.concatenate([ot[r - 1], o_h], axis=0)
                    cols = slice((h // 2) * pair_rows, (h // 2 + 1) * pair_rows)
                    o_ref[0, qi * tq:(qi + 1) * tq, cols] = pair_t.T.astype(BF16)
                    ot[r - 1] = ot[r] = None


def _mla_attn(qt, k_lat, k_ctx, vt_lat, vt_ctx):
    bsz, qk_w, seq = qt.shape
    ctx_len = k_ctx.shape[1]
    v_w = vt_lat.shape[1]
    o_w = MLA_HEADS * MLA_V
    tq = min(ATT_TQ, seq)
    tstep = min(ATT_TQ * ATT_SUBTILES, seq)
    assert seq % ATT_KB == 0 and ctx_len % ATT_KB == 0 and tstep % tq == 0
    return pl.pallas_call(
        _mla_attn_kernel,
        out_shape=jax.ShapeDtypeStruct((bsz, seq, o_w), BF16),
        grid=(bsz, seq // tstep),
        in_specs=[pl.BlockSpec((1, qk_w, tstep), lambda b, i: (b, 0, i)),
                  pl.BlockSpec((1, seq, qk_w), lambda b, i: (b, 0, 0)),
                  pl.BlockSpec((1, ctx_len, qk_w), lambda b, i: (b, 0, 0)),
                  pl.BlockSpec((1, v_w, seq), lambda b, i: (b, 0, 0)),
                  pl.BlockSpec((1, v_w, ctx_len), lambda b, i: (b, 0, 0))],
        out_specs=pl.BlockSpec((1, tstep, o_w), lambda b, i: (b, i, 0)),
        scratch_shapes=[pltpu.VMEM((2, seq + ctx_len, tq), F32)],
        compiler_params=_params("parallel", "parallel"),
        name="mla_attn",
    )(qt, k_lat, k_ctx, vt_lat, vt_ctx)


def _retention_kernel(rd_ref, q_ref, k_ref, v_ref, sg_ref, kc_ref, vc_ref, g_ref, o_ref, pf_ref, sb_ref):
    c = RET_CHUNK
    n_chunks = q_ref.shape[1] // c
    lgf = -jnp.exp(rd_ref[0, 0:1, :])
    lgb = -jnp.exp(rd_ref[0, 1:2, :])
    lgf1 = lgf[:, :RET_DV]
    lgb1 = lgb[:, :RET_DV]
    row = lax.broadcasted_iota(jnp.int32, (c, RET_DV), 0).astype(F32)
    zeta_f = jnp.exp(lgf1 * (c - 1.0 - row))
    zeta_b = jnp.exp(lgb1 * row)
    xi_f = jnp.exp(lgf1 * (row + 1.0))
    xi_b = jnp.exp(lgb1 * (c - row))
    gc_f = jnp.exp(lgf1 * float(c))
    gc_b = jnp.exp(lgb1 * float(c))
    diff = (lax.broadcasted_iota(jnp.int32, (c, c), 0) - lax.broadcasted_iota(jnp.int32, (c, c), 1)).astype(F32)
    dmat = jnp.where(diff >= 0.0, jnp.exp(lgf * jnp.maximum(diff, 0.0)), jnp.exp(lgb * jnp.maximum(-diff, 0.0)))

    def kv_outer(kk, vv):
        vf = vv.astype(F32)
        v2 = jnp.concatenate([vf * zeta_f, vf * zeta_b], axis=1).astype(BF16)
        return lax.dot_general(kk, v2, TN_DIMS, preferred_element_type=F32)

    p_ctx = kv_outer(kc_ref[0], vc_ref[0])
    sf = p_ctx[:, :RET_DV]
    sb = p_ctx[:, RET_DV:]

    for ci in reversed(range(n_chunks)):
        rows = slice(ci * c, (ci + 1) * c)
        p = kv_outer(k_ref[0, rows, :], v_ref[0, rows, :])
        pf_ref[ci] = p[:, :RET_DV]
        sb_ref[ci] = sb
        sb = gc_b * sb + p[:, RET_DV:]

    g = g_ref[...]
    for ci in range(n_chunks):
        rows = slice(ci * c, (ci + 1) * c)
        qc = q_ref[0, rows, :]
        s = lax.dot_general(qc, k_ref[0, rows, :], NT_DIMS, preferred_element_type=F32)
        inner = _dot((s * dmat).astype(BF16), v_ref[0, rows, :])
        st = jnp.concatenate([sf, sb_ref[ci]], axis=1).astype(BF16)
        cross = _dot(qc, st)
        y = inner + cross[:, :RET_DV] * xi_f + cross[:, RET_DV:] * xi_b
        mu = jnp.mean(y, axis=-1, keepdims=True)
        d = y - mu
        var = jnp.mean(d * d, axis=-1, keepdims=True)
        yn = d * lax.rsqrt(var + EPS) * g
        o_ref[0, rows, :] = (sg_ref[0, rows, :].astype(F32) * yn).astype(BF16)
        sf = gc_f * sf + pf_ref[ci]


def _retention(rd, rq, rk, rv, sg, rk_ctx, rv_ctx, g_ret):
    bsz, seq, _ = rq.shape
    ctx_len = rk_ctx.shape[1]
    assert ctx_len == RET_CHUNK and seq % RET_CHUNK == 0

    def head(rows):
        return pl.BlockSpec((1, rows, RET_DK), lambda b, h: (b, 0, h))

    return pl.pallas_call(
        _retention_kernel,
        out_shape=jax.ShapeDtypeStruct((bsz, seq, RET_HEADS * RET_DV), BF16),
        grid=(bsz, RET_HEADS),
        in_specs=[pl.BlockSpec((1, 2, RET_CHUNK), lambda b, h: (h, 0, 0)),
                  head(seq), head(seq), head(seq), head(seq), head(ctx_len), head(ctx_len),
                  pl.BlockSpec((1, RET_DV), lambda b, h: (0, h))],
        out_specs=head(seq),
        scratch_shapes=[pltpu.VMEM((seq // RET_CHUNK, RET_DK, RET_DV), F32),
                        pltpu.VMEM((seq // RET_CHUNK, RET_DK, RET_DV), F32)],
        compiler_params=_params("parallel", "parallel"),
        name="retention",
    )(rd, rq, rk, rv, sg, rk_ctx, rv_ctx, g_ret)


def _ffn_kernel(x_ref, xp_ref, xn_ref, om_ref, omp_ref, omn_ref, or_ref, orp_ref, orn_ref,
                gt1_ref, sh_ref, sc_ref, gt_ref, g2_ref, gf_ref, wo_ref,
                wup_ref, cw_ref, cb_ref, wdn_ref, o_ref, hs_ref, acc_ref):
    tm = x_ref.shape[1]
    n_inner = wup_ref.shape[0]
    half = om_ref.shape[2]
    i = pl.program_id(1)

    def mix(o_mla, o_ret):
        return _dot(o_mla, wo_ref[0:half, :]) + _dot(o_ret, wo_ref[half:, :])

    def edge(before_ref, after_ref):
        return jnp.concatenate([before_ref[0].astype(F32)[BF16_ROWS - HALO:], after_ref[0].astype(F32)[:HALO]],
                               axis=0).astype(BF16)

    gt1 = gt1_ref[0]
    x = x_ref[0] + gt1 * mix(om_ref[0], or_ref[0])
    o_ref[0] = x
    halo = (jnp.concatenate([xp_ref[0], xn_ref[0]], axis=0)
            + gt1 * mix(edge(omp_ref, omn_ref), edge(orp_ref, orn_ref)))
    gain = g2_ref[...] * (1.0 + sc_ref[0])
    shift = sh_ref[0]
    hs_ref[0:tm, :] = (_rmsnorm(x, gain) + shift).astype(BF16)
    hs_ref[tm:tm + 2 * HALO, :] = (_rmsnorm(halo, gain) + shift).astype(BF16)
    has_prev = (i > 0).astype(F32)
    has_next = (i < pl.num_programs(1) - 1).astype(F32)
    row8 = lax.broadcasted_iota(jnp.int32, (HALO, 2 * FFN_FC), 0)
    acc_ref[...] = jnp.zeros_like(acc_ref)

    def up_proj(j):
        return _dot(hs_ref[...], wup_ref[j])

    def conv_act(u, j):
        um = u[0:tm]
        u_before = u[tm + HALO - 1:tm + HALO] * has_prev
        u_after = u[tm + HALO:tm + HALO + 1] * has_next
        up = pltpu.roll(um, 1, 0)
        up = jnp.concatenate([jnp.where(row8 == 0, u_before, up[0:HALO]), up[HALO:]], axis=0)
        un = pltpu.roll(um, tm - 1, 0)
        un = jnp.concatenate([un[:tm - HALO], jnp.where(row8 == HALO - 1, u_after, un[tm - HALO:])], axis=0)
        cw = cw_ref[j]
        cv = up * cw[0:1] + um * cw[1:2] + un * cw[2:3] + cb_ref[j]
        return (_silu(cv[:, FFN_FC:]) * cv[:, :FFN_FC]).astype(BF16)

    u = up_proj(0)
    pending = None
    for j in range(n_inner):
        u_next = up_proj(j + 1) if j + 1 < n_inner else None
        part = _dot(conv_act(u, j), wdn_ref[j])
        if pending is None and j + 1 < n_inner:
            pending = part
        else:
            acc_ref[...] += part if pending is None else pending + part
            pending = None
        u = u_next
    o_ref[0] = _rmsnorm(o_ref[0] + gt_ref[0] * acc_ref[...], gf_ref[...])


def _ffn(x, o_mla, o_ret, gt1, sh, sc, gt, g2, gf, wo, wup, cw, cb, wdn):
    bsz, seq, d = x.shape
    tm = min(FFN_TM, seq)
    mod_spec = pl.BlockSpec((1, 1, d), lambda b, i: (b, 0, 0))

    def tile(width):
        return pl.BlockSpec((1, tm, width), lambda b, i: (b, i, 0))

    def before(rows, width):
        return pl.BlockSpec((1, rows, width), lambda b, i: (b, jnp.maximum(i * (tm // rows) - 1, 0), 0))

    def after(rows, width):
        last = seq // rows - 1
        return pl.BlockSpec((1, rows, width), lambda b, i: (b, jnp.minimum((i + 1) * (tm // rows), last), 0))

    mw, rw = o_mla.shape[2], o_ret.shape[2]
    return pl.pallas_call(
        _ffn_kernel,
        out_shape=jax.ShapeDtypeStruct((bsz, seq, d), F32),
        grid=(bsz, seq // tm),
        in_specs=[tile(d), before(HALO, d), after(HALO, d),
                  tile(mw), before(BF16_ROWS, mw), after(BF16_ROWS, mw),
                  tile(rw), before(BF16_ROWS, rw), after(BF16_ROWS, rw),
                  mod_spec, mod_spec, mod_spec, mod_spec, _const_spec(g2.shape), _const_spec(gf.shape),
                  _const_spec(wo.shape),
                  _const_spec(wup.shape), _const_spec(cw.shape), _const_spec(cb.shape), _const_spec(wdn.shape)],
        out_specs=tile(d),
        scratch_shapes=[pltpu.VMEM((tm + 2 * HALO, d), BF16), pltpu.VMEM((tm, d), F32)],
        compiler_params=_params("parallel", "parallel"),
        name="ffn",
    )(x, x, x, o_mla, o_mla, o_mla, o_ret, o_ret, o_ret, gt1, sh, sc, gt, g2, gf, wo, wup, cw, cb, wdn)


def _rot_half_cols(w, half):
    shape = w.shape
    w = w.reshape(shape[:-1] + (shape[-1] // (2 * half), 2, half))
    return jnp.stack([-w[..., 1, :], w[..., 0, :]], axis=-2).reshape(shape)


def _rope_angles(pos, dim):
    inv = ROPE_BASE ** (-np.arange(0, dim, 2, dtype=np.float64) / dim)
    return pos.astype(np.float64)[:, None] * inv[None, :]


def _mla_tables(seq, with_pos):
    if not with_pos:
        return np.ones((seq, LANES), np.float32), np.zeros((seq, LANES), np.float32)
    pos = np.arange(seq)
    quarter = MLA_ROPE // 2
    ar = _rope_angles(pos // GRID_W, quarter)
    ac = _rope_angles(pos % GRID_W, quarter)
    ang = np.concatenate([ar, ar, ac, ac], axis=-1)
    pad = LANES - MLA_NOPE - MLA_ROPE
    cos = np.concatenate([np.ones((seq, MLA_NOPE)), np.cos(ang), np.ones((seq, pad))], axis=-1)
    sin = np.concatenate([np.zeros((seq, MLA_NOPE)), np.sin(ang), np.zeros((seq, pad))], axis=-1)
    return cos.astype(np.float32), sin.astype(np.float32)


def _ret_tables(seq, with_pos):
    if not with_pos:
        return np.ones((seq, RET_DK), np.float32), np.zeros((seq, RET_DK), np.float32)
    ang = _rope_angles(np.arange(seq), RET_DK)
    return (np.concatenate([np.cos(ang), np.cos(ang)], axis=-1).astype(np.float32),
            np.concatenate([-np.sin(ang), np.sin(ang)], axis=-1).astype(np.float32))


def _pad_heads(nope, rope):
    r = nope.shape[0]
    pad = jnp.zeros((r, MLA_HEADS, HEAD_PAD - MLA_NOPE - MLA_ROPE), nope.dtype)
    return jnp.concatenate([nope, rope, pad], axis=-1).reshape(r, MLA_HEADS * HEAD_PAD)


def _pair_up_kernel(a_ref, g_ref, o_ref):
    o_ref[0, :, :FFN_FC] = a_ref[0].astype(BF16)
    o_ref[0, :, FFN_FC:] = g_ref[0].astype(BF16)


def _pair_up_weights(w_up, layer, nj):
    d = w_up.shape[1]
    return pl.pallas_call(
        _pair_up_kernel,
        out_shape=jax.ShapeDtypeStruct((nj, d, 2 * FFN_FC), BF16),
        grid=(nj,),
        in_specs=[pl.BlockSpec((1, d, FFN_FC), lambda j: (layer, 0, j)),
                  pl.BlockSpec((1, d, FFN_FC), lambda j: (layer, 0, nj + j))],
        out_specs=pl.BlockSpec((1, d, 2 * FFN_FC), lambda j: (j, 0, 0)),
        compiler_params=_params("parallel"),
        name="pair_up_weights",
    )(w_up, w_up)


def _layout_weights(w_in, w_uq, w_ukv, w_up_stack, layer, conv_w, conv_b, w_down):
    d = w_in.shape[0]
    q_rank = w_uq.shape[0]
    o = np.cumsum([0, q_rank, w_ukv.shape[0], MLA_ROPE, RET_W, RET_W, RET_W, RET_W])
    w_cq, w_ckv, w_kpe, w_rq, w_rk, w_rv, w_rg = [w_in[:, o[t]:o[t + 1]] for t in range(7)]
    rot_half = MLA_ROPE // 4

    def place_rope(w):
        z0 = jnp.zeros((d, MLA_NOPE), w.dtype)
        z1 = jnp.zeros((d, LANES - MLA_NOPE - MLA_ROPE), w.dtype)
        return jnp.concatenate([z0, w, z1], axis=-1)

    win = jnp.concatenate([w_cq, w_ckv, place_rope(w_kpe), place_rope(_rot_half_cols(w_kpe, rot_half)),
                           w_rq, w_rk, w_rv, w_rg], axis=-1).astype(BF16)
    assert win.shape[1] == COLS_IN

    uq = w_uq.reshape(q_rank, MLA_HEADS, MLA_NOPE + MLA_ROPE)
    uq_n, uq_r = uq[..., :MLA_NOPE], uq[..., MLA_NOPE:]
    wuq = jnp.concatenate([_pad_heads(uq_n, uq_r),
                           _pad_heads(jnp.zeros_like(uq_n), _rot_half_cols(uq_r, rot_half))],
                          axis=-1).T.astype(BF16)

    ukv =w_ukv.reshape(w_ukv.shape[0], MLA_HEADS, MLA_NOPE + MLA_V)
    wk = _pad_heads(ukv[..., :MLA_NOPE], jnp.zeros(ukv.shape[:2] + (MLA_ROPE,), ukv.dtype)).astype(BF16)
    wv = ukv[..., MLA_NOPE:].reshape(w_ukv.shape[0], MLA_HEADS * MLA_V).T.astype(BF16)

    d_ff = w_down.shape[0]
    nj = d_ff // FFN_FC

    def pair_chunks(a):
        lead = a.shape[:-1]
        a = a.reshape(lead + (2, nj, FFN_FC))
        a = jnp.moveaxis(a, -2, 0)
        return a.reshape((nj,) + lead + (2 * FFN_FC,))

    wup = _pair_up_weights(w_up_stack, layer, nj)
    cw = pair_chunks(conv_w)
    cb = pair_chunks(conv_b[None, :])
    wdn = w_down.reshape(nj, FFN_FC, w_down.shape[1]).astype(BF16)
    return win, wuq, wk, wv, wup, cw, cb, wdn


def kernel(x, c, ctx, c_ctx, w_ada, b_ada, g_norm1, w_in, g_q, w_uq, g_kv, w_ukv, ret_decay, g_ret,
           w_out, g_norm2, w_up, conv_w, conv_b, w_down, g_final):
    bsz, seq, d = x.shape
    ctx_len = ctx.shape[1]
    depth = w_ada.shape[0]
    assert depth == 1, "single-layer block"
    l = 0

    rows = -(-(bsz + 1) // 8) * 8
    cs = jnp.concatenate([c, c_ctx[None, :], jnp.zeros((rows - bsz - 1, d), F32)], axis=0)
    mod = _adaln(cs, w_ada, b_ada[l][None, :], l)
    sh1, sc1, gt1, sh2, sc2, gt2 = [mod[:bsz, t * d:(t + 1) * d][:, None, :] for t in range(6)]
    shc1 = jnp.broadcast_to(mod[bsz, 0:d][None, None, :], (bsz, 1, d))
    scc1 = jnp.broadcast_to(mod[bsz, d:2 * d][None, None, :], (bsz, 1, d))

    win, wuq, wk, wv, wup, cw, cb, wdn = _layout_weights(w_in[l], w_uq[l], w_ukv[l], w_up, l, conv_w[l],
                                                       conv_b[l], w_down[l])
    g1 = g_norm1[l][None, :]
    gq = g_q[l][None, :]
    gkv = g_kv[l][None, :]

    cm, sm = _mla_tables(seq, True)
    cr, sr = _ret_tables(seq, True)
    q, k, v, rq, rk, rv, sg = _preproj(x, sh1, sc1, g1, win, gq, wuq, gkv, wk, wv, cm, sm, cr, sr,
                                       with_queries=True)
    cm0, sm0 = _mla_tables(ctx_len, False)
    cr0, sr0 = _ret_tables(ctx_len, False)
    k_c, v_c, rk_c, rv_c = _preproj(ctx, shc1, scc1, g1, win, gq, wuq, gkv, wk, wv, cm0, sm0, cr0, sr0,
                                    with_queries=False)

    o_mla = _mla_attn(q, k, k_c, v, v_c)
    rd = jnp.broadcast_to(jnp.transpose(ret_decay[l])[:, :, None], (RET_HEADS, 2, RET_CHUNK)).astype(F32)
    o_ret = _retention(rd, rq, rk, rv, sg, rk_c, rv_c, g_ret[l][None, :])

    return _ffn(x, o_mla, o_ret, gt1, sh2, sc2, gt2, g_norm2[l][None, :], g_final[None, :],
                w_out[l].astype(BF16), wup, cw, cb, wdn)
```

```python
import functools

import jax
import jax.numpy as jnp
import numpy as np
from jax import lax
from jax.experimental import pallas as pl
from jax.experimental.pallas import tpu as pltpu

F32 = jnp.float32
BF16 = jnp.bfloat16

GRID_W = 64
MLA_HEADS = 8
MLA_NOPE = 64
MLA_ROPE = 32
MLA_V = 64
MLA_SCALE = (MLA_NOPE + MLA_ROPE) ** -0.5
Q_SCALE = MLA_SCALE * float(np.log2(np.e))
SCORE_FLOOR = -1e30
RET_HEADS = 4
RET_DK = 128
RET_DV = 128
ROPE_BASE = 10000.0
EPS = 1e-6

LANES = 128
MXU_DIM = 256
BF16_ROWS = 16
VT_ROWS = MLA_V + BF16_ROWS
VMEM_LIMIT_BYTES = 56 * 1024 * 1024

HEAD_PAD = LANES
PRE_TM = 512
ATT_TQ = 512
ATT_SUBTILES = 2
ATT_KB = MXU_DIM
ATT_PV_LAG = 4
RET_CHUNK = 256
FFN_TM = 512
FFN_FC = 256
FFN_SUBTILES = 4
HALO = 8

NT_DIMS = (((1,), (1,)), ((), ()))
TN_DIMS = (((0,), (0,)), ((), ()))


def _dot(a, b):
    return jnp.dot(a, b, preferred_element_type=F32)


def _rmsnorm(x, g):
    return x * lax.rsqrt(jnp.mean(x * x, axis=-1, keepdims=True) + EPS) * g


def _silu(x):
    return x * (1.0 / (1.0 + jnp.exp(-x)))


def _params(*semantics):
    return pltpu.CompilerParams(dimension_semantics=semantics, vmem_limit_bytes=VMEM_LIMIT_BYTES)


def _const_spec(shape):
    zeros = (0,) * len(shape)
    return pl.BlockSpec(shape, lambda *_: zeros, pipeline_mode=pl.Buffered(1))


def _adaln_kernel(c_ref, w_ref, b_ref, o_ref):
    a = _silu(c_ref[...])
    o_ref[...] = _dot(a.astype(BF16), w_ref[0].astype(BF16)) + b_ref[...]


def _adaln(cs, w, b, layer):
    rows, d = cs.shape
    n = w.shape[2]
    tn = 1024
    return pl.pallas_call(
        _adaln_kernel,
        out_shape=jax.ShapeDtypeStruct((rows, n), F32),
        grid=(n // tn,),
        in_specs=[pl.BlockSpec((rows, d), lambda j: (0, 0)),
                  pl.BlockSpec((1, d, tn), lambda j: (layer, 0, j)),
                  pl.BlockSpec((1, tn), lambda j: (0, j))],
        out_specs=pl.BlockSpec((rows, tn), lambda j: (0, j)),
        compiler_params=_params("parallel"),
        name="adaln",
    )(cs, w, b)


COL_CQ = 0
COL_CKV = 256
COL_KPE = 512
COL_KPE_ROT = 640
COL_RQ = 768
COL_RK = 1280
COL_RV = 1792
COL_RG = 2304
COLS_IN = 2816
RET_W = RET_HEADS * RET_DK


def _preproj_kernel(with_queries, x_ref, sh_ref, sc_ref, g1_ref, win_ref, gq_ref, wuq_ref,
                    gkv_ref, wk_ref, wv_ref, cm_ref, sm_ref, cmt_ref, smt_ref, cr_ref, sr_ref, *out_refs):
    if with_queries:
        q_ref, k_ref, v_ref, rq_ref, rk_ref, rv_ref, sg_ref = out_refs
    else:
        k_ref, v_ref, rk_ref, rv_ref = out_refs
    x = x_ref[0]
    h = _rmsnorm(x, g1_ref[...] * (1.0 + sc_ref[0])) + sh_ref[0]
    proj = _dot(h.astype(BF16), win_ref[...])
    cm = cm_ref[...]
    sm = sm_ref[...]
    cr = cr_ref[...]
    sr = sr_ref[...]

    ckvn = _rmsnorm(proj[:, COL_CKV:COL_CKV + 256], gkv_ref[...]).astype(BF16)
    kn = _dot(ckvn, wk_ref[...])
    vt = lax.dot_general(wv_ref[...], ckvn, NT_DIMS, preferred_element_type=F32).astype(BF16)
    ones_rows = jnp.ones((BF16_ROWS, vt.shape[1]), BF16)
    for hd in range(MLA_HEADS):
        v_ref[0, hd * VT_ROWS:hd * VT_ROWS + MLA_V, :] = vt[hd * MLA_V:(hd + 1) * MLA_V]
        v_ref[0, hd * VT_ROWS + MLA_V:(hd + 1) * VT_ROWS, :] = ones_rows
    kpe = proj[:, COL_KPE:COL_KPE + LANES] * cm + proj[:, COL_KPE_ROT:COL_KPE_ROT + LANES] * sm
    for hd in range(MLA_HEADS):
        sl = slice(hd * HEAD_PAD, (hd + 1) * HEAD_PAD)
        k_ref[0, :, sl] = (kn[:, sl] + kpe).astype(BF16)

    for hd in range(RET_HEADS):
        sl = slice(hd * RET_DK, (hd + 1) * RET_DK)
        b = proj[:, COL_RK + hd * RET_DK:COL_RK + (hd + 1) * RET_DK]
        rk_ref[0, :, sl] = ((b * cr + pltpu.roll(b, RET_DK // 2, 1) * sr) * (RET_DK ** -0.5)).astype(BF16)
    rv_ref[0] = proj[:, COL_RV:COL_RV + RET_W].astype(BF16)

    if with_queries:
        cqn = _rmsnorm(proj[:, COL_CQ:COL_CQ + 256], gq_ref[...]).astype(BF16)
        qa = lax.dot_general(wuq_ref[...], cqn, NT_DIMS, preferred_element_type=F32)
        cmt = cmt_ref[...]
        smt = smt_ref[...]
        rot0 = MLA_HEADS * HEAD_PAD
        for hd in range(MLA_HEADS):
            sl = slice(hd * HEAD_PAD, (hd + 1) * HEAD_PAD)
            a = qa[sl, :]
            b = qa[rot0 + hd * HEAD_PAD:rot0 + (hd + 1) * HEAD_PAD, :]
            q_ref[0, sl, :] = ((a * cmt + b * smt) * Q_SCALE).astype(BF16)
        for hd in range(RET_HEADS):
            sl = slice(hd * RET_DK, (hd + 1) * RET_DK)
            a = proj[:, COL_RQ + hd * RET_DK:COL_RQ + (hd + 1) * RET_DK]
            rq_ref[0, :, sl] = (a * cr + pltpu.roll(a, RET_DK // 2, 1) * sr).astype(BF16)
        sg_ref[0] = _silu(proj[:, COL_RG:COL_RG + RET_W]).astype(BF16)


def _preproj(x, sh, sc, g1, win, gq, wuq, gkv, wk, wv, cm, sm, cr, sr, *, with_queries):
    bsz, seq, d = x.shape
    tm = min(PRE_TM, seq)
    qk_w = MLA_HEADS * HEAD_PAD
    v_w = MLA_HEADS * VT_ROWS
    cmt, smt = cm.T, sm.T
    tabt_spec = pl.BlockSpec((LANES, tm), lambda b, i: (0, i))
    qt_shape = jax.ShapeDtypeStruct((bsz, qk_w, seq), BF16)
    qt_spec = pl.BlockSpec((1, qk_w, tm), lambda b, i: (b, 0, i))

    def tok(width):
        return pl.BlockSpec((1, tm, width), lambda b, i: (b, i, 0))

    def out(width):
        return jax.ShapeDtypeStruct((bsz, seq, width), BF16)

    mod_spec = pl.BlockSpec((1, 1, d), lambda b, i: (b, 0, 0))
    tab_spec = pl.BlockSpec((tm, LANES), lambda b, i: (i, 0))
    vt_shape = jax.ShapeDtypeStruct((bsz, v_w, seq), BF16)
    vt_spec = pl.BlockSpec((1, v_w, tm), lambda b, i: (b, 0, i))
    if with_queries:
        out_shape = [qt_shape, out(qk_w), vt_shape, out(RET_W), out(RET_W), out(RET_W), out(RET_W)]
        out_specs = [qt_spec, tok(qk_w), vt_spec, tok(RET_W), tok(RET_W), tok(RET_W), tok(RET_W)]
    else:
        out_shape = [out(qk_w), vt_shape, out(RET_W), out(RET_W)]
        out_specs = [tok(qk_w), vt_spec, tok(RET_W), tok(RET_W)]
    return pl.pallas_call(
        functools.partial(_preproj_kernel, with_queries),
        out_shape=out_shape,
        grid=(bsz, seq // tm),
        in_specs=[tok(d), mod_spec, mod_spec, _const_spec(g1.shape), _const_spec(win.shape),
                  _const_spec(gq.shape), _const_spec(wuq.shape), _const_spec(gkv.shape),
                  _const_spec(wk.shape), _const_spec(wv.shape), tab_spec, tab_spec, tabt_spec, tabt_spec,
                  tab_spec, tab_spec],
        out_specs=out_specs,
        compiler_params=_params("parallel", "parallel"),
        name="preproj_q" if with_queries else "preproj_ctx",
    )(x, sh, sc, g1, win, gq, wuq, gkv, wk, wv, cm, sm, cmt, smt, cr, sr)


def _mla_attn_kernel(q_ref, kl_ref, kc_ref, vl_ref, vc_ref, o_ref, s_ref):
    kb = ATT_KB
    n_lat = kl_ref.shape[1] // kb
    n_blk = n_lat + kc_ref.shape[1] // kb
    pair_rows = 2 * MLA_V

    def k_block(h, b):
        hs = slice(h * HEAD_PAD, (h + 1) * HEAD_PAD)
        if b < n_lat:
            return kl_ref[0, b * kb:(b + 1) * kb, hs]
        return kc_ref[0, (b - n_lat) * kb:(b - n_lat + 1) * kb, hs]

    def vt_block(h, b):
        rs = slice(h * VT_ROWS, (h + 1) * VT_ROWS)
        if b < n_lat:
            return vl_ref[0, rs, b * kb:(b + 1) * kb]
        return vc_ref[0, rs, (b - n_lat) * kb:(b - n_lat + 1) * kb]

    tq = s_ref.shape[2]
    n_rows = (q_ref.shape[2] // tq) * MLA_HEADS
    n_units = n_rows * n_blk
    exp_lag = n_blk
    pv_lag = exp_lag + ATT_PV_LAG
    m_run = [None] * n_rows
    ot = [None] * n_rows
    p_blocks = {}
    for t in range(n_units + pv_lag):
        bm = None
        if t < n_units:
            r, b = divmod(t, n_blk)
            qi, h = divmod(r, MLA_HEADS)
            qt = q_ref[0, h * HEAD_PAD:(h + 1) * HEAD_PAD, qi * tq:(qi + 1) * tq]
            s = _dot(k_block(h, b), qt)
            s_ref[r % 2, b * kb:(b + 1) * kb, :] = s
            bm = jnp.max(s, axis=0, keepdims=True)
            m_run[r] = bm if m_run[r] is None else jnp.maximum(m_run[r], bm)
        e = t - exp_lag
        if 0 <= e < n_units:
            r, b = divmod(e, n_blk)
            m_use = m_run[r]
            if bm is not None:
                m_use = jnp.maximum(m_use, jnp.minimum(bm, SCORE_FLOOR))
            p_blocks[e] = jnp.exp2(s_ref[r % 2, b * kb:(b + 1) * kb, :] - m_use).astype(BF16)
        v = t - pv_lag
        if 0 <= v < n_units:
            r, b = divmod(v, n_blk)
            qi, h = divmod(r, MLA_HEADS)
            pv = _dot(vt_block(h, b), p_blocks.pop(v))
            ot[r] = pv if ot[r] is None else ot[r] + pv
            if b == n_blk - 1:
                o_h = ot[r][:MLA_V] * (1.0 / ot[r][MLA_V:MLA_V + 1])
                ot[r] = o_h
                if h % 2 == 1:
                    pair_t = jnp.concatenate([ot[r - 1], o_h], axis=0)
                    cols = slice((h // 2) * pair_rows, (h // 2 + 1) * pair_rows)
                    o_ref[0, qi * tq:(qi + 1) * tq, cols] = pair_t.T.astype(BF16)
                    ot[r - 1] = ot[r] = None


def _mla_attn(qt, k_lat, k_ctx, vt_lat, vt_ctx):
    bsz, qk_w, seq = qt.shape
    ctx_len = k_ctx.shape[1]
    v_w = vt_lat.shape[1]
    o_w = MLA_HEADS * MLA_V
    tq = min(ATT_TQ, seq)
    tstep = min(ATT_TQ * ATT_SUBTILES, seq)
    assert seq % ATT_KB == 0 and ctx_len % ATT_KB == 0 and tstep % tq == 0
    return pl.pallas_call(
        _mla_attn_kernel,
        out_shape=jax.ShapeDtypeStruct((bsz, seq, o_w), BF16),
        grid=(bsz, seq // tstep),
        in_specs=[pl.BlockSpec((1, qk_w, tstep), lambda b, i: (b, 0, i)),
                  pl.BlockSpec((1, seq, qk_w), lambda b, i: (b, 0, 0)),
                  pl.BlockSpec((1, ctx_len, qk_w), lambda b, i: (b, 0, 0)),
                  pl.BlockSpec((1, v_w, seq), lambda b, i: (b, 0, 0)),
                  pl.BlockSpec((1, v_w, ctx_len), lambda b, i: (b, 0, 0))],
        out_specs=pl.BlockSpec((1, tstep, o_w), lambda b, i: (b, i, 0)),
        scratch_shapes=[pltpu.VMEM((2, seq + ctx_len, tq), F32)],
        compiler_params=_params("parallel", "parallel"),
        name="mla_attn",
    )(qt, k_lat, k_ctx, vt_lat, vt_ctx)


def _retention_kernel(rd_ref, q_ref, k_ref, v_ref, sg_ref, kc_ref, vc_ref, g_ref, o_ref, pf_ref, sb_ref):
    c = RET_CHUNK
    n_chunks = q_ref.shape[1] // c
    lgf = -jnp.exp(rd_ref[0, 0:1, :])
    lgb = -jnp.exp(rd_ref[0, 1:2, :])
    lgf1 = lgf[:, :RET_DV]
    lgb1 = lgb[:, :RET_DV]
    row = lax.broadcasted_iota(jnp.int32, (c, RET_DV), 0).astype(F32)
    zeta_f = jnp.exp(lgf1 * (c - 1.0 - row))
    zeta_b = jnp.exp(lgb1 * row)
    xi_f = jnp.exp(lgf1 * (row + 1.0))
    xi_b = jnp.exp(lgb1 * (c - row))
    gc_f = jnp.exp(lgf1 * float(c))
    gc_b = jnp.exp(lgb1 * float(c))
    diff = (lax.broadcasted_iota(jnp.int32, (c, c), 0) - lax.broadcasted_iota(jnp.int32, (c, c), 1)).astype(F32)
    dmat = jnp.where(diff >= 0.0, jnp.exp(lgf * jnp.maximum(diff, 0.0)), jnp.exp(lgb * jnp.maximum(-diff, 0.0)))

    def kv_outer(kk, vv):
        vf = vv.astype(F32)
        v2 = jnp.concatenate([vf * zeta_f, vf * zeta_b], axis=1).astype(BF16)
        return lax.dot_general(kk, v2, TN_DIMS, preferred_element_type=F32)

    p_ctx = kv_outer(kc_ref[0], vc_ref[0])
    sf = p_ctx[:, :RET_DV]
    sb = p_ctx[:, RET_DV:]

    for ci in reversed(range(n_chunks)):
        rows = slice(ci * c, (ci + 1) * c)
        p = kv_outer(k_ref[0, rows, :], v_ref[0, rows, :])
        pf_ref[ci] = p[:, :RET_DV]
        sb_ref[ci] = sb
        sb = gc_b * sb + p[:, RET_DV:]

    g = g_ref[...]
    for ci in range(n_chunks):
        rows = slice(ci * c, (ci + 1) * c)
        qc = q_ref[0, rows, :]
        s = lax.dot_general(qc, k_ref[0, rows, :], NT_DIMS, preferred_element_type=F32)
        inner = _dot((s * dmat).astype(BF16), v_ref[0, rows, :])
        st = jnp.concatenate([sf, sb_ref[ci]], axis=1).astype(BF16)
        cross = _dot(qc, st)
        y = inner + cross[:, :RET_DV] * xi_f + cross[:, RET_DV:] * xi_b
        mu = jnp.mean(y, axis=-1, keepdims=True)
        d = y - mu
        var = jnp.mean(d * d, axis=-1, keepdims=True)
        yn = d * lax.rsqrt(var + EPS) * g
        o_ref[0, rows, :] = (sg_ref[0, rows, :].astype(F32) * yn).astype(BF16)
        sf = gc_f * sf + pf_ref[ci]


def _retention(rd, rq, rk, rv, sg, rk_ctx, rv_ctx, g_ret):
    bsz, seq, _ = rq.shape
    ctx_len = rk_ctx.shape[1]
    assert ctx_len == RET_CHUNK and seq % RET_CHUNK == 0

    def head(rows):
        return pl.BlockSpec((1, rows, RET_DK), lambda b, h: (b, 0, h))

    return pl.pallas_call(
        _retention_kernel,
        out_shape=jax.ShapeDtypeStruct((bsz, seq, RET_HEADS * RET_DV), BF16),
        grid=(bsz, RET_HEADS),
        in_specs=[pl.BlockSpec((1, 2, RET_CHUNK), lambda b, h: (h, 0, 0)),
                  head(seq), head(seq), head(seq), head(seq), head(ctx_len), head(ctx_len),
                  pl.BlockSpec((1, RET_DV), lambda b, h: (0, h))],
        out_specs=head(seq),
        scratch_shapes=[pltpu.VMEM((seq // RET_CHUNK, RET_DK, RET_DV), F32),
                        pltpu.VMEM((seq // RET_CHUNK, RET_DK, RET_DV), F32)],
        compiler_params=_params("parallel", "parallel"),
        name="retention",
    )(rd, rq, rk, rv, sg, rk_ctx, rv_ctx, g_ret)


def _ffn_kernel(x_ref, xp_ref, xn_ref, om_ref, omp_ref, omn_ref, or_ref, orp_ref, orn_ref,
                gt1_ref, sh_ref, sc_ref, gt_ref, g2_ref, gf_ref, wo_ref,
                wup_ref, cw_ref, cb_ref, wdn_ref, o_ref, hs_ref, acc_ref):
    tm = x_ref.shape[1]
    n_inner = wup_ref.shape[0]
    half = om_ref.shape[2]
    i = pl.program_id(1)

    def mix(o_mla, o_ret):
        return _dot(o_mla, wo_ref[0:half, :]) + _dot(o_ret, wo_ref[half:, :])

    def edge(before_ref, after_ref):
        return jnp.concatenate([before_ref[0].astype(F32)[BF16_ROWS - HALO:], after_ref[0].astype(F32)[:HALO]],
                               axis=0).astype(BF16)

    gt1 = gt1_ref[0]
    x = x_ref[0] + gt1 * mix(om_ref[0], or_ref[0])
    o_ref[0] = x
    halo = (jnp.concatenate([xp_ref[0], xn_ref[0]], axis=0)
            + gt1 * mix(edge(omp_ref, omn_ref), edge(orp_ref, orn_ref)))
    gain = g2_ref[...] * (1.0 + sc_ref[0])
    shift = sh_ref[0]
    hs_ref[0:tm, :] = (_rmsnorm(x, gain) + shift).astype(BF16)
    hs_ref[tm:tm + 2 * HALO, :] = (_rmsnorm(halo, gain) + shift).astype(BF16)
    has_prev = (i > 0).astype(F32)
    has_next = (i < pl.num_programs(1) - 1).astype(F32)
    row8 = lax.broadcasted_iota(jnp.int32, (HALO, 2 * FFN_FC), 0)
    acc_ref[...] = jnp.zeros_like(acc_ref)

    n_sub = FFN_SUBTILES
    rp = tm // n_sub
    last = n_sub - 1

    def up_proj(j, r):
        rows = slice(r * rp, (r + 1) * rp + (2 * HALO if r == last else 0))
        return _dot(hs_ref[rows, :], wup_ref[j])

    def conv_act(u, j, r):
        um = u[r][0:rp]
        if r == 0:
            u_before = u[last][rp + HALO - 1:rp + HALO] * has_prev
        else:
            u_before = u[r - 1][rp - 1:rp]
        if r == last:
            u_after = u[last][rp + HALO:rp + HALO + 1] * has_next
        else:
            u_after = u[r + 1][0:1]
        up = pltpu.roll(um, 1, 0)
        up = jnp.concatenate([jnp.where(row8 == 0, u_before, up[0:HALO]), up[HALO:]], axis=0)
        un = pltpu.roll(um, rp - 1, 0)
        un = jnp.concatenate([un[:rp - HALO], jnp.where(row8 == HALO - 1, u_after, un[rp - HALO:])], axis=0)
        cw = cw_ref[j]
        cv = up * cw[0:1] + um * cw[1:2] + un * cw[2:3] + cb_ref[j]
        return (_silu(cv[:, FFN_FC:]) * cv[:, :FFN_FC]).astype(BF16)

    u = [up_proj(0, r) for r in range(n_sub)]
    pending = [None] * n_sub
    for j in range(n_inner):
        u_next = [None] * n_sub
        for r in range(n_sub):
            if j + 1 < n_inner:
                u_next[r] = up_proj(j + 1, r)
            part = _dot(conv_act(u, j, r), wdn_ref[j])
            if pending[r] is None and j + 1 < n_inner:
                pending[r] = part
            else:
                acc_ref[r * rp:(r + 1) * rp, :] += part if pending[r] is None else pending[r] + part
                pending[r] = None
        u = u_next
    o_ref[0] = _rmsnorm(o_ref[0] + gt_ref[0] * acc_ref[...], gf_ref[...])


def _ffn(x, o_mla, o_ret, gt1, sh, sc, gt, g2, gf, wo, wup, cw, cb, wdn):
    bsz, seq, d = x.shape
    tm = min(FFN_TM, seq)
    mod_spec = pl.BlockSpec((1, 1, d), lambda b, i: (b, 0, 0))

    def tile(width):
        return pl.BlockSpec((1, tm, width), lambda b, i: (b, i, 0))

    def before(rows, width):
        return pl.BlockSpec((1, rows, width), lambda b, i: (b, jnp.maximum(i * (tm // rows) - 1, 0), 0))

    def after(rows, width):
        last = seq // rows - 1
        return pl.BlockSpec((1, rows, width), lambda b, i: (b, jnp.minimum((i + 1) * (tm // rows), last), 0))

    mw, rw = o_mla.shape[2], o_ret.shape[2]
    return pl.pallas_call(
        _ffn_kernel,
        out_shape=jax.ShapeDtypeStruct((bsz, seq, d), F32),
        grid=(bsz, seq // tm),
        in_specs=[tile(d), before(HALO, d), after(HALO, d),
                  tile(mw), before(BF16_ROWS, mw), after(BF16_ROWS, mw),
                  tile(rw), before(BF16_ROWS, rw), after(BF16_ROWS, rw),
                  mod_spec, mod_spec, mod_spec, mod_spec, _const_spec(g2.shape), _const_spec(gf.shape),
                  _const_spec(wo.shape),
                  _const_spec(wup.shape), _const_spec(cw.shape), _const_spec(cb.shape), _const_spec(wdn.shape)],
        out_specs=tile(d),
        scratch_shapes=[pltpu.VMEM((tm + 2 * HALO, d), BF16), pltpu.VMEM((tm, d), F32)],
        compiler_params=_params("parallel", "parallel"),
        name="ffn",
    )(x, x, x, o_mla, o_mla, o_mla, o_ret, o_ret, o_ret, gt1, sh, sc, gt, g2, gf, wo, wup, cw, cb, wdn)


def _rot_half_cols(w, half):
    shape = w.shape
    w = w.reshape(shape[:-1] + (shape[-1] // (2 * half), 2, half))
    return jnp.stack([-w[..., 1, :], w[..., 0, :]], axis=-2).reshape(shape)


def _rope_angles(pos, dim):
    inv = ROPE_BASE ** (-np.arange(0, dim, 2, dtype=np.float64) / dim)
    return pos.astype(np.float64)[:, None] * inv[None, :]


def _mla_tables(seq, with_pos):
    if not with_pos:
        return np.ones((seq, LANES), np.float32), np.zeros((seq, LANES), np.float32)
    pos = np.arange(seq)
    quarter = MLA_ROPE // 2
    ar = _rope_angles(pos // GRID_W, quarter)
    ac = _rope_angles(pos % GRID_W, quarter)
    ang = np.concatenate([ar, ar, ac, ac], axis=-1)
    pad = LANES - MLA_NOPE - MLA_ROPE
    cos = np.concatenate([np.ones((seq, MLA_NOPE)), np.cos(ang), np.ones((seq, pad))], axis=-1)
    sin = np.concatenate([np.zeros((seq, MLA_NOPE)), np.sin(ang), np.zeros((seq, pad))], axis=-1)
    return cos.astype(np.float32), sin.astype(np.float32)


def _ret_tables(seq, with_pos):
    if not with_pos:
        return np.ones((seq, RET_DK), np.float32), np.zeros((seq, RET_DK), np.float32)
    ang = _rope_angles(np.arange(seq), RET_DK)
    return (np.concatenate([np.cos(ang), np.cos(ang)], axis=-1).astype(np.float32),
            np.concatenate([-np.sin(ang), np.sin(ang)], axis=-1).astype(np.float32))


def _pad_heads(nope, rope):
    r = nope.shape[0]
    pad = jnp.zeros((r, MLA_HEADS, HEAD_PAD - MLA_NOPE - MLA_ROPE), nope.dtype)
    return jnp.concatenate([nope, rope, pad], axis=-1).reshape(r, MLA_HEADS * HEAD_PAD)


def _pair_up_kernel(a_ref, g_ref, o_ref):
    o_ref[0, :, :FFN_FC] = a_ref[0].astype(BF16)
    o_ref[0, :, FFN_FC:] = g_ref[0].astype(BF16)


def _pair_up_weights(w_up, layer, nj):
    d = w_up.shape[1]
    return pl.pallas_call(
        _pair_up_kernel,
        out_shape=jax.ShapeDtypeStruct((nj, d, 2 * FFN_FC), BF16),
        grid=(nj,),
        in_specs=[pl.BlockSpec((1, d, FFN_FC), lambda j: (layer, 0, j)),
                  pl.BlockSpec((1, d, FFN_FC), lambda j: (layer, 0, nj + j))],
        out_specs=pl.BlockSpec((1, d, 2 * FFN_FC), lambda j: (j, 0, 0)),
        compiler_params=_params("parallel"),
        name="pair_up_weights",
    )(w_up, w_up)


def _rope_placement():
    place = np.zeros((LANES, LANES), np.float32)
    rot = np.zeros((LANES, LANES), np.float32)
    half = MLA_ROPE // 4
    for c in range(MLA_ROPE):
        place[c, MLA_NOPE + c] = 1.0
        g, w = divmod(c, 2 * half)
        if w < half:
            rot[g * 2 * half + w + half, MLA_NOPE + c] = -1.0
        else:
            rot[g * 2 * half + w - half, MLA_NOPE + c] = 1.0
    return place, rot


def _win_kernel(w_ref, place_ref, rot_ref, o_ref):
    w = w_ref[0]
    o_ref[:, 0:COL_KPE] = w[:, 0:COL_KPE].astype(BF16)
    blk = w[:, COL_KPE:COL_KPE + LANES].astype(BF16)
    o_ref[:, COL_KPE:COL_KPE + LANES] = _dot(blk, place_ref[...]).astype(BF16)
    o_ref[:, COL_KPE_ROT:COL_KPE_ROT + LANES] = _dot(blk, rot_ref[...]).astype(BF16)
    o_ref[:, COL_RQ:] = w[:, COL_KPE + MLA_ROPE:].astype(BF16)


def _layout_w_in(w_in, layer):
    d, n = w_in.shape[1], w_in.shape[2]
    assert n - MLA_ROPE + 2 * LANES == COLS_IN
    rows = 256
    place, rot = _rope_placement()
    return pl.pallas_call(
        _win_kernel,
        out_shape=jax.ShapeDtypeStruct((d, COLS_IN), BF16),
        grid=(d // rows,),
        in_specs=[pl.BlockSpec((1, rows, n), lambda i: (layer, i, 0)),
                  _const_spec(place.shape), _const_spec(rot.shape)],
        out_specs=pl.BlockSpec((rows, COLS_IN), lambda i: (i, 0)),
        compiler_params=_params("parallel"),
        name="layout_w_in",
    )(w_in, place.astype(BF16), rot.astype(BF16))


def _layout_weights(w_in_stack, w_uq, w_ukv, w_up_stack, layer, conv_w, conv_b, w_down):
    q_rank = w_uq.shape[0]
    rot_half = MLA_ROPE // 4
    win = _layout_w_in(w_in_stack, layer)

    uq = w_uq.reshape(q_rank, MLA_HEADS, MLA_NOPE + MLA_ROPE)
    uq_n, uq_r = uq[..., :MLA_NOPE], uq[..., MLA_NOPE:]
    wuq = jnp.concatenate([_pad_heads(uq_n, uq_r),
                           _pad_heads(jnp.zeros_like(uq_n), _rot_half_cols(uq_r, rot_half))],
                          axis=-1).T.astype(BF16)

    ukv =w_ukv.reshape(w_ukv.shape[0], MLA_HEADS, MLA_NOPE + MLA_V)
    wk = _pad_heads(ukv[..., :MLA_NOPE], jnp.zeros(ukv.shape[:2] + (MLA_ROPE,), ukv.dtype)).astype(BF16)
    wv = ukv[..., MLA_NOPE:].reshape(w_ukv.shape[0], MLA_HEADS * MLA_V).T.astype(BF16)

    d_ff = w_down.shape[0]
    nj = d_ff // FFN_FC

    def pair_chunks(a):
        lead = a.shape[:-1]
        a = a.reshape(lead + (2, nj, FFN_FC))
        a = jnp.moveaxis(a, -2, 0)
        return a.reshape((nj,) + lead + (2 * FFN_FC,))

    wup = _pair_up_weights(w_up_stack, layer, nj)
    cw = pair_chunks(conv_w)
    cb = pair_chunks(conv_b[None, :])
    wdn = w_down.reshape(nj, FFN_FC, w_down.shape[1]).astype(BF16)
    return win, wuq, wk, wv, wup, cw, cb, wdn


def kernel(x, c, ctx, c_ctx, w_ada, b_ada, g_norm1, w_in, g_q, w_uq, g_kv, w_ukv, ret_decay, g_ret,
           w_out, g_norm2, w_up, conv_w, conv_b, w_down, g_final):
    bsz, seq, d = x.shape
    ctx_len = ctx.shape[1]
    depth = w_ada.shape[0]
    assert depth == 1, "single-layer block"
    l = 0

    rows = -(-(bsz + 1) // 8) * 8
    cs = jnp.concatenate([c, c_ctx[None, :], jnp.zeros((rows - bsz - 1, d), F32)], axis=0)
    mod = _adaln(cs, w_ada, b_ada[l][None, :], l)
    sh1, sc1, gt1, sh2, sc2, gt2 = [mod[:bsz, t * d:(t + 1) * d][:, None, :] for t in range(6)]
    shc1 = jnp.broadcast_to(mod[bsz, 0:d][None, None, :], (bsz, 1, d))
    scc1 = jnp.broadcast_to(mod[bsz, d:2 * d][None, None, :], (bsz, 1, d))

    win, wuq, wk, wv, wup, cw, cb, wdn = _layout_weights(w_in, w_uq[l], w_ukv[l], w_up, l, conv_w[l],
                                                       conv_b[l], w_down[l])
    g1 = g_norm1[l][None, :]
    gq = g_q[l][None, :]
    gkv = g_kv[l][None, :]

    cm, sm = _mla_tables(seq, True)
    cr, sr = _ret_tables(seq, True)
    q, k, v, rq, rk, rv, sg = _preproj(x, sh1, sc1, g1, win, gq, wuq, gkv, wk, wv, cm, sm, cr, sr,
                                       with_queries=True)
    cm0, sm0 = _mla_tables(ctx_len, False)
    cr0, sr0 = _ret_tables(ctx_len, False)
    k_c, v_c, rk_c, rv_c = _preproj(ctx, shc1, scc1, g1, win, gq, wuq, gkv, wk, wv, cm0, sm0, cr0, sr0,
                                    with_queries=False)

    o_mla = _mla_attn(q, k, k_c, v, v_c)
    rd = jnp.broadcast_to(jnp.transpose(ret_decay[l])[:, :, None], (RET_HEADS, 2, RET_CHUNK)).astype(F32)
    o_ret = _retention(rd, rq, rk, rv, sg, rk_c, rv_c, g_ret[l][None, :])

    return _ffn(x, o_mla, o_ret, gt1, sh2, sc2, gt2, g_norm2[l][None, :], g_final[None, :],
                w_out[l].astype(BF16), wup, cw, cb, wdn)
```

```python
import functools

import jax
import jax.numpy as jnp
import numpy as np
from jax import lax
from jax.experimental import pallas as pl
from jax.experimental.pallas import tpu as pltpu

F32 = jnp.float32
BF16 = jnp.bfloat16

GRID_W = 64
MLA_HEADS = 8
MLA_NOPE = 64
MLA_ROPE = 32
MLA_V = 64
MLA_SCALE = (MLA_NOPE + MLA_ROPE) ** -0.5
Q_SCALE = MLA_SCALE * float(np.log2(np.e))
SCORE_FLOOR = -1e30
RET_HEADS = 4
RET_DK = 128
RET_DV = 128
ROPE_BASE = 10000.0
EPS = 1e-6

LANES = 128
MXU_DIM = 256
BF16_ROWS = 16
VT_ROWS = MLA_V + BF16_ROWS
VMEM_LIMIT_BYTES = 56 * 1024 * 1024

HEAD_PAD = LANES
PRE_TM = 512
PRE_SUBTILES = 2
ATT_TQ = 512
ATT_SUBTILES = 2
ATT_KB = MXU_DIM
ATT_PV_LAG = 4
RET_CHUNK = 256
FFN_TM = 512
FFN_FC = 256
FFN_SUBTILES = 4
HALO = 8

NT_DIMS = (((1,), (1,)), ((), ()))
TN_DIMS = (((0,), (0,)), ((), ()))


def _dot(a, b):
    return jnp.dot(a, b, preferred_element_type=F32)


def _rmsnorm(x, g):
    return x * lax.rsqrt(jnp.mean(x * x, axis=-1, keepdims=True) + EPS) * g


def _silu(x):
    return x * (1.0 / (1.0 + jnp.exp(-x)))


def _params(*semantics):
    return pltpu.CompilerParams(dimension_semantics=semantics, vmem_limit_bytes=VMEM_LIMIT_BYTES)


def _const_spec(shape):
    zeros = (0,) * len(shape)
    return pl.BlockSpec(shape, lambda *_: zeros, pipeline_mode=pl.Buffered(1))


def _adaln_kernel(c_ref, w_ref, b_ref, o_ref):
    a = _silu(c_ref[...])
    o_ref[...] = _dot(a.astype(BF16), w_ref[0].astype(BF16)) + b_ref[...]


def _adaln(cs, w, b, layer):
    rows, d = cs.shape
    n = w.shape[2]
    tn = 1024
    return pl.pallas_call(
        _adaln_kernel,
        out_shape=jax.ShapeDtypeStruct((rows, n), F32),
        grid=(n // tn,),
        in_specs=[pl.BlockSpec((rows, d), lambda j: (0, 0)),
                  pl.BlockSpec((1, d, tn), lambda j: (layer, 0, j)),
                  pl.BlockSpec((1, tn), lambda j: (0, j))],
        out_specs=pl.BlockSpec((rows, tn), lambda j: (0, j)),
        compiler_params=_params("parallel"),
        name="adaln",
    )(cs, w, b)


COL_CQ = 0
COL_CKV = 256
COL_KPE = 512
COL_KPE_ROT = 640
COL_RQ = 768
COL_RK = 1280
COL_RV = 1792
COL_RG = 2304
COLS_IN = 2816
RET_W = RET_HEADS * RET_DK


def _preproj_kernel(with_queries, x_ref, sh_ref, sc_ref, g1_ref, win_ref, gq_ref, wuq_ref,
                    gkv_ref, wk_ref, wv_ref, cm_ref, sm_ref, cmt_ref, smt_ref, cr_ref, sr_ref, *out_refs):
    if with_queries:
        q_ref, k_ref, v_ref, rq_ref, rk_ref, rv_ref, sg_ref = out_refs
    else:
        k_ref, v_ref, rk_ref, rv_ref = out_refs
    n_sub = max(1, min(PRE_SUBTILES, x_ref.shape[1] // MXU_DIM))
    rp = x_ref.shape[1] // n_sub
    gain = g1_ref[...] * (1.0 + sc_ref[0])
    shift = sh_ref[0]

    projs = []
    for r in range(n_sub):
        rows = slice(r * rp, (r + 1) * rp)
        h = _rmsnorm(x_ref[0, rows, :], gain) + shift
        projs.append(_dot(h.astype(BF16), win_ref[...]))

    for r in range(n_sub):
        rows = slice(r * rp, (r + 1) * rp)
        proj = projs[r]
        cm = cm_ref[rows, :]
        sm = sm_ref[rows, :]
        cr = cr_ref[rows, :]
        sr = sr_ref[rows, :]

        ckvn = _rmsnorm(proj[:, COL_CKV:COL_CKV + 256], gkv_ref[...]).astype(BF16)
        kn = _dot(ckvn, wk_ref[...])
        vt = lax.dot_general(wv_ref[...], ckvn, NT_DIMS, preferred_element_type=F32).astype(BF16)
        ones_rows = jnp.ones((BF16_ROWS, rp), BF16)
        for hd in range(MLA_HEADS):
            v_ref[0, hd * VT_ROWS:hd * VT_ROWS + MLA_V, rows] = vt[hd * MLA_V:(hd + 1) * MLA_V]
            v_ref[0, hd * VT_ROWS + MLA_V:(hd + 1) * VT_ROWS, rows] = ones_rows
        kpe = proj[:, COL_KPE:COL_KPE + LANES] * cm + proj[:, COL_KPE_ROT:COL_KPE_ROT + LANES] * sm
        for hd in range(MLA_HEADS):
            sl = slice(hd * HEAD_PAD, (hd + 1) * HEAD_PAD)
            k_ref[0, rows, sl] = (kn[:, sl] + kpe).astype(BF16)

        for hd in range(RET_HEADS):
            sl = slice(hd * RET_DK, (hd + 1) * RET_DK)
            b = proj[:, COL_RK + hd * RET_DK:COL_RK + (hd + 1) * RET_DK]
            rk_ref[0, rows, sl] = ((b * cr + pltpu.roll(b, RET_DK // 2, 1) * sr) * (RET_DK ** -0.5)).astype(BF16)
        rv_ref[0, rows, :] = proj[:, COL_RV:COL_RV + RET_W].astype(BF16)

        if with_queries:
            cqn = _rmsnorm(proj[:, COL_CQ:COL_CQ + 256], gq_ref[...]).astype(BF16)
            qa = lax.dot_general(wuq_ref[...], cqn, NT_DIMS, preferred_element_type=F32)
            cmt = cmt_ref[:, rows]
            smt = smt_ref[:, rows]
            rot0 = MLA_HEADS * HEAD_PAD
            for hd in range(MLA_HEADS):
                sl = slice(hd * HEAD_PAD, (hd + 1) * HEAD_PAD)
                a = qa[sl, :]
                b = qa[rot0 + hd * HEAD_PAD:rot0 + (hd + 1) * HEAD_PAD, :]
                q_ref[0, sl, rows] = ((a * cmt + b * smt) * Q_SCALE).astype(BF16)
            for hd in range(RET_HEADS):
                sl = slice(hd * RET_DK, (hd + 1) * RET_DK)
                a = proj[:, COL_RQ + hd * RET_DK:COL_RQ + (hd + 1) * RET_DK]
                rq_ref[0, rows, sl] = (a * cr + pltpu.roll(a, RET_DK // 2, 1) * sr).astype(BF16)
            sg_ref[0, rows, :] = _silu(proj[:, COL_RG:COL_RG + RET_W]).astype(BF16)


def _preproj(x, sh, sc, g1, win, gq, wuq, gkv, wk, wv, cm, sm, cr, sr, *, with_queries):
    bsz, seq, d = x.shape
    tm = min(PRE_TM, seq)
    qk_w = MLA_HEADS * HEAD_PAD
    v_w = MLA_HEADS * VT_ROWS
    cmt, smt = cm.T, sm.T
    tabt_spec = pl.BlockSpec((LANES, tm), lambda b, i: (0, i))
    qt_shape = jax.ShapeDtypeStruct((bsz, qk_w, seq), BF16)
    qt_spec = pl.BlockSpec((1, qk_w, tm), lambda b, i: (b, 0, i))

    def tok(width):
        return pl.BlockSpec((1, tm, width), lambda b, i: (b, i, 0))

    def out(width):
        return jax.ShapeDtypeStruct((bsz, seq, width), BF16)

    mod_spec = pl.BlockSpec((1, 1, d), lambda b, i: (b, 0, 0))
    tab_spec = pl.BlockSpec((tm, LANES), lambda b, i: (i, 0))
    vt_shape = jax.ShapeDtypeStruct((bsz, v_w, seq), BF16)
    vt_spec = pl.BlockSpec((1, v_w, tm), lambda b, i: (b, 0, i))
    if with_queries:
        out_shape = [qt_shape, out(qk_w), vt_shape, out(RET_W), out(RET_W), out(RET_W), out(RET_W)]
        out_specs = [qt_spec, tok(qk_w), vt_spec, tok(RET_W), tok(RET_W), tok(RET_W), tok(RET_W)]
    else:
        out_shape = [out(qk_w), vt_shape, out(RET_W), out(RET_W)]
        out_specs = [tok(qk_w), vt_spec, tok(RET_W), tok(RET_W)]
    return pl.pallas_call(
        functools.partial(_preproj_kernel, with_queries),
        out_shape=out_shape,
        grid=(bsz, seq // tm),
        in_specs=[tok(d), mod_spec, mod_spec, _const_spec(g1.shape), _const_spec(win.shape),
                  _const_spec(gq.shape), _const_spec(wuq.shape), _const_spec(gkv.shape),
                  _const_spec(wk.shape), _const_spec(wv.shape), tab_spec, tab_spec, tabt_spec, tabt_spec,
                  tab_spec, tab_spec],
        out_specs=out_specs,
        compiler_params=_params("parallel", "parallel"),
        name="preproj_q" if with_queries else "preproj_ctx",
    )(x, sh, sc, g1, win, gq, wuq, gkv, wk, wv, cm, sm, cmt, smt, cr, sr)


def _mla_attn_kernel(q_ref, kl_ref, kc_ref, vl_ref, vc_ref, o_ref, s_ref):
    lat_len, ctx_len = kl_ref.shape[1], kc_ref.shape[1]
    kb_ctx = min(ATT_KB, ctx_len)
    blocks = ([(False, k0, ATT_KB) for k0 in range(0, lat_len, ATT_KB)]
              + [(True, k0, kb_ctx) for k0 in range(0, ctx_len, kb_ctx)])
    n_blk = len(blocks)
    pair_rows = 2 * MLA_V

    def s_rows(b):
        from_ctx, k0, size = blocks[b]
        k0 += lat_len if from_ctx else 0
        return slice(k0, k0 + size)

    def k_block(h, b):
        from_ctx, k0, size = blocks[b]
        ref = kc_ref if from_ctx else kl_ref
        return ref[0, k0:k0 + size, h * HEAD_PAD:(h + 1) * HEAD_PAD]

    def vt_block(h, b):
        from_ctx, k0, size = blocks[b]
        ref = vc_ref if from_ctx else vl_ref
        return ref[0, h * VT_ROWS:(h + 1) * VT_ROWS, k0:k0 + size]

    tq = s_ref.shape[2]
    n_rows = (q_ref.shape[2] // tq) * MLA_HEADS
    n_units = n_rows * n_blk
    exp_lag = n_blk
    pv_lag = exp_lag + ATT_PV_LAG
    m_run = [None] * n_rows
    ot = [None] * n_rows
    p_blocks = {}
    for t in range(n_units + pv_lag):
        bm = None
        if t < n_units:
            r, b = divmod(t, n_blk)
            qi, h = divmod(r, MLA_HEADS)
            qt = q_ref[0, h * HEAD_PAD:(h + 1) * HEAD_PAD, qi * tq:(qi + 1) * tq]
            s = _dot(k_block(h, b), qt)
            s_ref[r % 2, s_rows(b), :] = s
            bm = jnp.max(s, axis=0, keepdims=True)
            m_run[r] = bm if m_run[r] is None else jnp.maximum(m_run[r], bm)
        e = t - exp_lag
        if 0 <= e < n_units:
            r, b = divmod(e, n_blk)
            m_use = m_run[r]
            if bm is not None:
                m_use = jnp.maximum(m_use, jnp.minimum(bm, SCORE_FLOOR))
            p_blocks[e] = jnp.exp2(s_ref[r % 2, s_rows(b), :] - m_use).astype(BF16)
        v = t - pv_lag
        if 0 <= v < n_units:
            r, b = divmod(v, n_blk)
            qi, h = divmod(r, MLA_HEADS)
            pv = _dot(vt_block(h, b), p_blocks.pop(v))
            ot[r] = pv if ot[r] is None else ot[r] + pv
            if b == n_blk - 1:
                o_h = ot[r][:MLA_V] * (1.0 / ot[r][MLA_V:MLA_V + 1])
                ot[r] = o_h
                if h % 2 == 1:
                    pair_t = jnp.concatenate([ot[r - 1], o_h], axis=0)
                    cols = slice((h // 2) * pair_rows, (h // 2 + 1) * pair_rows)
                    o_ref[0, qi * tq:(qi + 1) * tq, cols] = pair_t.T.astype(BF16)
                    ot[r - 1] = ot[r] = None


def _mla_attn(qt, k_lat, k_ctx, vt_lat, vt_ctx):
    bsz, qk_w, seq = qt.shape
    ctx_len = k_ctx.shape[1]
    v_w = vt_lat.shape[1]
    o_w = MLA_HEADS * MLA_V
    tq = min(ATT_TQ, seq)
    tstep = min(ATT_TQ * ATT_SUBTILES, seq)
    assert seq % ATT_KB == 0 and ctx_len % min(ATT_KB, ctx_len) == 0 and tstep % tq == 0
    return pl.pallas_call(
        _mla_attn_kernel,
        out_shape=jax.ShapeDtypeStruct((bsz, seq, o_w), BF16),
        grid=(bsz, seq // tstep),
        in_specs=[pl.BlockSpec((1, qk_w, tstep), lambda b, i: (b, 0, i)),
                  pl.BlockSpec((1, seq, qk_w), lambda b, i: (b, 0, 0)),
                  pl.BlockSpec((1, ctx_len, qk_w), lambda b, i: (b, 0, 0)),
                  pl.BlockSpec((1, v_w, seq), lambda b, i: (b, 0, 0)),
                  pl.BlockSpec((1, v_w, ctx_len), lambda b, i: (b, 0, 0))],
        out_specs=pl.BlockSpec((1, tstep, o_w), lambda b, i: (b, i, 0)),
        scratch_shapes=[pltpu.VMEM((2, seq + ctx_len, tq), F32)],
        compiler_params=_params("parallel", "parallel"),
        name="mla_attn",
    )(qt, k_lat, k_ctx, vt_lat, vt_ctx)


def _retention_kernel(rd_ref, q_ref, k_ref, v_ref, sg_ref, kc_ref, vc_ref, g_ref, o_ref, pf_ref, sb_ref):
    c = RET_CHUNK
    n_chunks = q_ref.shape[1] // c
    lgf = -jnp.exp(rd_ref[0, 0:1, :])
    lgb = -jnp.exp(rd_ref[0, 1:2, :])
    lgf1 = lgf[:, :RET_DV]
    lgb1 = lgb[:, :RET_DV]
    row = lax.broadcasted_iota(jnp.int32, (c, RET_DV), 0).astype(F32)
    zeta_f = jnp.exp(lgf1 * (c - 1.0 - row))
    zeta_b = jnp.exp(lgb1 * row)
    xi_f = jnp.exp(lgf1 * (row + 1.0))
    xi_b = jnp.exp(lgb1 * (c - row))
    gc_f = jnp.exp(lgf1 * float(c))
    gc_b = jnp.exp(lgb1 * float(c))
    diff = (lax.broadcasted_iota(jnp.int32, (c, c), 0) - lax.broadcasted_iota(jnp.int32, (c, c), 1)).astype(F32)
    dmat = jnp.where(diff >= 0.0, jnp.exp(lgf * jnp.maximum(diff, 0.0)), jnp.exp(lgb * jnp.maximum(-diff, 0.0)))

    def kv_outer(kk, vv):
        vf = vv.astype(F32)
        v2 = jnp.concatenate([vf * zeta_f, vf * zeta_b], axis=1).astype(BF16)
        return lax.dot_general(kk, v2, TN_DIMS, preferred_element_type=F32)

    p_ctx = kv_outer(kc_ref[0], vc_ref[0])
    sf = p_ctx[:, :RET_DV]
    sb = p_ctx[:, RET_DV:]

    for ci in reversed(range(n_chunks)):
        rows = slice(ci * c, (ci + 1) * c)
        p = kv_outer(k_ref[0, rows, :], v_ref[0, rows, :])
        pf_ref[ci] = p[:, :RET_DV]
        sb_ref[ci] = sb
        sb = gc_b * sb + p[:, RET_DV:]

    g = g_ref[...]
    for ci in range(n_chunks):
        rows = slice(ci * c, (ci + 1) * c)
        qc = q_ref[0, rows, :]
        s = lax.dot_general(qc, k_ref[0, rows, :], NT_DIMS, preferred_element_type=F32)
        inner = _dot((s * dmat).astype(BF16), v_ref[0, rows, :])
        st = jnp.concatenate([sf, sb_ref[ci]], axis=1).astype(BF16)
        cross = _dot(qc, st)
        y = inner + cross[:, :RET_DV] * xi_f + cross[:, RET_DV:] * xi_b
        mu = jnp.mean(y, axis=-1, keepdims=True)
        d = y - mu
        var = jnp.mean(d * d, axis=-1, keepdims=True)
        yn = d * lax.rsqrt(var + EPS) * g
        o_ref[0, rows, :] = (sg_ref[0, rows, :].astype(F32) * yn).astype(BF16)
        sf = gc_f * sf + pf_ref[ci]


def _retention(rd, rq, rk, rv, sg, rk_ctx, rv_ctx, g_ret):
    bsz, seq, _ = rq.shape
    ctx_len = rk_ctx.shape[1]
    assert ctx_len == RET_CHUNK and seq % RET_CHUNK == 0

    def head(rows):
        return pl.BlockSpec((1, rows, RET_DK), lambda b, h: (b, 0, h))

    return pl.pallas_call(
        _retention_kernel,
        out_shape=jax.ShapeDtypeStruct((bsz, seq, RET_HEADS * RET_DV), BF16),
        grid=(bsz, RET_HEADS),
        in_specs=[pl.BlockSpec((1, 2, RET_CHUNK), lambda b, h: (h, 0, 0)),
                  head(seq), head(seq), head(seq), head(seq), head(ctx_len), head(ctx_len),
                  pl.BlockSpec((1, RET_DV), lambda b, h: (0, h))],
        out_specs=head(seq),
        scratch_shapes=[pltpu.VMEM((seq // RET_CHUNK, RET_DK, RET_DV), F32),
                        pltpu.VMEM((seq // RET_CHUNK, RET_DK, RET_DV), F32)],
        compiler_params=_params("parallel", "parallel"),
        name="retention",
    )(rd, rq, rk, rv, sg, rk_ctx, rv_ctx, g_ret)


def _ffn_kernel(x_ref, xp_ref, xn_ref, om_ref, omp_ref, omn_ref, or_ref, orp_ref, orn_ref,
                gt1_ref, sh_ref, sc_ref, gt_ref, g2_ref, gf_ref, wo_ref,
                wup_ref, cw_ref, cb_ref, wdn_ref, o_ref, hs_ref, os_ref, acc_ref):
    tm = x_ref.shape[1]
    n_inner = wup_ref.shape[0]
    half = om_ref.shape[2]
    i = pl.program_id(1)

    def edge(before_ref, after_ref):
        return jnp.concatenate([before_ref[0].astype(F32)[BF16_ROWS - HALO:], after_ref[0].astype(F32)[:HALO]],
                               axis=0).astype(BF16)

    gt1 = gt1_ref[0]
    gain = g2_ref[...] * (1.0 + sc_ref[0])
    shift = sh_ref[0]
    has_prev = (i > 0).astype(F32)
    has_next = (i < pl.num_programs(1) - 1).astype(F32)
    row8 = lax.broadcasted_iota(jnp.int32, (HALO, 2 * FFN_FC), 0)

    n_sub = FFN_SUBTILES
    rp = tm // n_sub
    last = n_sub - 1

    def up_proj(j, r):
        rows = slice(r * rp, (r + 1) * rp + (2 * HALO if r == last else 0))
        return _dot(hs_ref[rows, :], wup_ref[j])

    def conv_act(u, j, r):
        um = u[r][0:rp]
        if r == 0:
            u_before = u[last][rp + HALO - 1:rp + HALO] * has_prev
        else:
            u_before = u[r - 1][rp - 1:rp]
        if r == last:
            u_after = u[last][rp + HALO:rp + HALO + 1] * has_next
        else:
            u_after = u[r + 1][0:1]
        up = pltpu.roll(um, 1, 0)
        up = jnp.concatenate([jnp.where(row8 == 0, u_before, up[0:HALO]), up[HALO:]], axis=0)
        un = pltpu.roll(um, rp - 1, 0)
        un = jnp.concatenate([un[:rp - HALO], jnp.where(row8 == HALO - 1, u_after, un[rp - HALO:])], axis=0)
        cw = cw_ref[j]
        cv = up * cw[0:1] + um * cw[1:2] + un * cw[2:3] + cb_ref[j]
        return (_silu(cv[:, FFN_FC:]) * cv[:, :FFN_FC]).astype(BF16)

    os_ref[0:tm, 0:half] = om_ref[0]
    os_ref[0:tm, half:] = or_ref[0]
    os_ref[tm:, 0:half] = edge(omp_ref, omn_ref)
    os_ref[tm:, half:] = edge(orp_ref, orn_ref)
    mixed = _dot(os_ref[...], wo_ref[...])
    x = x_ref[0] + gt1 * mixed[0:tm]
    o_ref[0] = x
    halo = jnp.concatenate([xp_ref[0], xn_ref[0]], axis=0) + gt1 * mixed[tm:]
    hs_ref[tm:tm + 2 * HALO, :] = (_rmsnorm(halo, gain) + shift).astype(BF16)
    u = []
    for r in range(n_sub):
        rows = slice(r * rp, (r + 1) * rp)
        hs_ref[rows, :] = (_rmsnorm(x[rows], gain) + shift).astype(BF16)
        u.append(up_proj(0, r))

    pending = [None] * n_sub
    acc_live = [False] * n_sub
    for j in range(n_inner):
        u_next = [None] * n_sub
        for r in range(n_sub):
            rows = slice(r * rp, (r + 1) * rp)
            if j + 1 < n_inner:
                u_next[r] = up_proj(j + 1, r)
            part = _dot(conv_act(u, j, r), wdn_ref[j])
            if pending[r] is None and j + 1 < n_inner:
                pending[r] = part
                continue
            total = part if pending[r] is None else pending[r] + part
            pending[r] = None
            if acc_live[r]:
                total = acc_ref[rows, :] + total
            if j + 1 < n_inner:
                acc_ref[rows, :] = total
                acc_live[r] = True
            else:
                o_ref[0, rows, :] = _rmsnorm(o_ref[0, rows, :] + gt_ref[0] * total, gf_ref[...])
        u = u_next


def _ffn(x, o_mla, o_ret, gt1, sh, sc, gt, g2, gf, wo, wup, cw, cb, wdn):
    bsz, seq, d = x.shape
    tm = min(FFN_TM, seq)
    mod_spec = pl.BlockSpec((1, 1, d), lambda b, i: (b, 0, 0))

    def tile(width):
        return pl.BlockSpec((1, tm, width), lambda b, i: (b, i, 0))

    def before(rows, width):
        return pl.BlockSpec((1, rows, width), lambda b, i: (b, jnp.maximum(i * (tm // rows) - 1, 0), 0))

    def after(rows, width):
        last = seq // rows - 1
        return pl.BlockSpec((1, rows, width), lambda b, i: (b, jnp.minimum((i + 1) * (tm // rows), last), 0))

    mw, rw = o_mla.shape[2], o_ret.shape[2]
    return pl.pallas_call(
        _ffn_kernel,
        out_shape=jax.ShapeDtypeStruct((bsz, seq, d), F32),
        grid=(bsz, seq // tm),
        in_specs=[tile(d), before(HALO, d), after(HALO, d),
                  tile(mw), before(BF16_ROWS, mw), after(BF16_ROWS, mw),
                  tile(rw), before(BF16_ROWS, rw), after(BF16_ROWS, rw),
                  mod_spec, mod_spec, mod_spec, mod_spec, _const_spec(g2.shape), _const_spec(gf.shape),
                  _const_spec(wo.shape),
                  _const_spec(wup.shape), _const_spec(cw.shape), _const_spec(cb.shape), _const_spec(wdn.shape)],
        out_specs=tile(d),
        scratch_shapes=[pltpu.VMEM((tm + 2 * HALO, d), BF16), pltpu.VMEM((tm + 2 * HALO, mw + rw), BF16),
                        pltpu.VMEM((tm, d), F32)],
        compiler_params=_params("parallel", "parallel"),
        name="ffn",
    )(x, x, x, o_mla, o_mla, o_mla, o_ret, o_ret, o_ret, gt1, sh, sc, gt, g2, gf, wo, wup, cw, cb, wdn)


def _rot_half_cols(w, half):
    shape = w.shape
    w = w.reshape(shape[:-1] + (shape[-1] // (2 * half), 2, half))
    return jnp.stack([-w[..., 1, :], w[..., 0, :]], axis=-2).reshape(shape)


def _rope_angles(pos, dim):
    inv = ROPE_BASE ** (-np.arange(0, dim, 2, dtype=np.float64) / dim)
    return pos.astype(np.float64)[:, None] * inv[None, :]


def _mla_tables(seq, with_pos):
    if not with_pos:
        return np.ones((seq, LANES), np.float32), np.zeros((seq, LANES), np.float32)
    pos = np.arange(seq)
    quarter = MLA_ROPE // 2
    ar = _rope_angles(pos // GRID_W, quarter)
    ac = _rope_angles(pos % GRID_W, quarter)
    ang = np.concatenate([ar, ar, ac, ac], axis=-1)
    pad = LANES - MLA_NOPE - MLA_ROPE
    cos = np.concatenate([np.ones((seq, MLA_NOPE)), np.cos(ang), np.ones((seq, pad))], axis=-1)
    sin = np.concatenate([np.zeros((seq, MLA_NOPE)), np.sin(ang), np.zeros((seq, pad))], axis=-1)
    return cos.astype(np.float32), sin.astype(np.float32)


def _ret_tables(seq, with_pos):
    if not with_pos:
        return np.ones((seq, RET_DK), np.float32), np.zeros((seq, RET_DK), np.float32)
    ang = _rope_angles(np.arange(seq), RET_DK)
    return (np.concatenate([np.cos(ang), np.cos(ang)], axis=-1).astype(np.float32),
            np.concatenate([-np.sin(ang), np.sin(ang)], axis=-1).astype(np.float32))


def _pad_heads(nope, rope):
    r = nope.shape[0]
    pad = jnp.zeros((r, MLA_HEADS, HEAD_PAD - MLA_NOPE - MLA_ROPE), nope.dtype)
    return jnp.concatenate([nope, rope, pad], axis=-1).reshape(r, MLA_HEADS * HEAD_PAD)


def _pair_up_kernel(a_ref, g_ref, o_ref):
    o_ref[0, :, :FFN_FC] = a_ref[0].astype(BF16)
    o_ref[0, :, FFN_FC:] = g_ref[0].astype(BF16)


def _pair_up_weights(w_up, layer, nj):
    d = w_up.shape[1]
    return pl.pallas_call(
        _pair_up_kernel,
        out_shape=jax.ShapeDtypeStruct((nj, d, 2 * FFN_FC), BF16),
        grid=(nj,),
        in_specs=[pl.BlockSpec((1, d, FFN_FC), lambda j: (layer, 0, j)),
                  pl.BlockSpec((1, d, FFN_FC), lambda j: (layer, 0, nj + j))],
        out_specs=pl.BlockSpec((1, d, 2 * FFN_FC), lambda j: (j, 0, 0)),
        compiler_params=_params("parallel"),
        name="pair_up_weights",
    )(w_up, w_up)


def _rope_placement():
    place = np.zeros((LANES, LANES), np.float32)
    rot = np.zeros((LANES, LANES), np.float32)
    half = MLA_ROPE // 4
    for c in range(MLA_ROPE):
        place[c, MLA_NOPE + c] = 1.0
        g, w = divmod(c, 2 * half)
        if w < half:
            rot[g * 2 * half + w + half, MLA_NOPE + c] = -1.0
        else:
            rot[g * 2 * half + w - half, MLA_NOPE + c] = 1.0
    return place, rot


def _win_kernel(w_ref, place_ref, rot_ref, o_ref):
    w = w_ref[0]
    o_ref[:, 0:COL_KPE] = w[:, 0:COL_KPE].astype(BF16)
    blk = w[:, COL_KPE:COL_KPE + LANES].astype(BF16)
    o_ref[:, COL_KPE:COL_KPE + LANES] = _dot(blk, place_ref[...]).astype(BF16)
    o_ref[:, COL_KPE_ROT:COL_KPE_ROT + LANES] = _dot(blk, rot_ref[...]).astype(BF16)
    o_ref[:, COL_RQ:] = w[:, COL_KPE + MLA_ROPE:].astype(BF16)


def _layout_w_in(w_in, layer):
    d, n = w_in.shape[1], w_in.shape[2]
    assert n - MLA_ROPE + 2 * LANES == COLS_IN
    rows = 256
    place, rot = _rope_placement()
    return pl.pallas_call(
        _win_kernel,
        out_shape=jax.ShapeDtypeStruct((d, COLS_IN), BF16),
        grid=(d // rows,),
        in_specs=[pl.BlockSpec((1, rows, n), lambda i: (layer, i, 0)),
                  _const_spec(place.shape), _const_spec(rot.shape)],
        out_specs=pl.BlockSpec((rows, COLS_IN), lambda i: (i, 0)),
        compiler_params=_params("parallel"),
        name="layout_w_in",
    )(w_in, place.astype(BF16), rot.astype(BF16))


def _layout_weights(w_in_stack, w_uq, w_ukv, w_up_stack, layer, conv_w, conv_b, w_down):
    q_rank = w_uq.shape[0]
    rot_half = MLA_ROPE // 4
    win = _layout_w_in(w_in_stack, layer)

    uq = w_uq.reshape(q_rank, MLA_HEADS, MLA_NOPE + MLA_ROPE)
    uq_n, uq_r = uq[..., :MLA_NOPE], uq[..., MLA_NOPE:]
    wuq = jnp.concatenate([_pad_heads(uq_n, uq_r),
                           _pad_heads(jnp.zeros_like(uq_n), _rot_half_cols(uq_r, rot_half))],
                          axis=-1).T.astype(BF16)

    ukv =w_ukv.reshape(w_ukv.shape[0], MLA_HEADS, MLA_NOPE + MLA_V)
    wk = _pad_heads(ukv[..., :MLA_NOPE], jnp.zeros(ukv.shape[:2] + (MLA_ROPE,), ukv.dtype)).astype(BF16)
    wv = ukv[..., MLA_NOPE:].reshape(w_ukv.shape[0], MLA_HEADS * MLA_V).T.astype(BF16)

    d_ff = w_down.shape[0]
    nj = d_ff // FFN_FC

    def pair_chunks(a):
        lead = a.shape[:-1]
        a = a.reshape(lead + (2, nj, FFN_FC))
        a = jnp.moveaxis(a, -2, 0)
        return a.reshape((nj,) + lead + (2 * FFN_FC,))

    wup = _pair_up_weights(w_up_stack, layer, nj)
    cw = pair_chunks(conv_w)
    cb = pair_chunks(conv_b[None, :])
    wdn = w_down.reshape(nj, FFN_FC, w_down.shape[1]).astype(BF16)
    return win, wuq, wk, wv, wup, cw, cb, wdn


def kernel(x, c, ctx, c_ctx, w_ada, b_ada, g_norm1, w_in, g_q, w_uq, g_kv, w_ukv, ret_decay, g_ret,
           w_out, g_norm2, w_up, conv_w, conv_b, w_down, g_final):
    bsz, seq, d = x.shape
    ctx_len = ctx.shape[1]
    depth = w_ada.shape[0]
    assert depth == 1, "single-layer block"
    l = 0

    rows = -(-(bsz + 1) // 8) * 8
    cs = jnp.concatenate([c, c_ctx[None, :], jnp.zeros((rows - bsz - 1, d), F32)], axis=0)
    mod = _adaln(cs, w_ada, b_ada[l][None, :], l)
    sh1, sc1, gt1, sh2, sc2, gt2 = [mod[:bsz, t * d:(t + 1) * d][:, None, :] for t in range(6)]
    shc1 = jnp.broadcast_to(mod[bsz, 0:d][None, None, :], (bsz, 1, d))
    scc1 = jnp.broadcast_to(mod[bsz, d:2 * d][None, None, :], (bsz, 1, d))

    win, wuq, wk, wv, wup, cw, cb, wdn = _layout_weights(w_in, w_uq[l], w_ukv[l], w_up, l, conv_w[l],
                                                       conv_b[l], w_down[l])
    g1 = g_norm1[l][None, :]
    gq = g_q[l][None, :]
    gkv = g_kv[l][None, :]

    cm, sm = _mla_tables(seq, True)
    cr, sr = _ret_tables(seq, True)
    q, k, v, rq, rk, rv, sg = _preproj(x, sh1, sc1, g1, win, gq, wuq, gkv, wk, wv, cm, sm, cr, sr,
                                       with_queries=True)
    cm0, sm0 = _mla_tables(ctx_len, False)
    cr0, sr0 = _ret_tables(ctx_len, False)
    k_c, v_c, rk_c, rv_c = _preproj(ctx, shc1, scc1, g1, win, gq, wuq, gkv, wk, wv, cm0, sm0, cr0, sr0,
                                    with_queries=False)

    o_mla = _mla_attn(q, k, k_c, v, v_c)
    rd = jnp.broadcast_to(jnp.transpose(ret_decay[l])[:, :, None], (RET_HEADS, 2, RET_CHUNK)).astype(F32)
    o_ret = _retention(rd, rq, rk, rv, sg, rk_c, rv_c, g_ret[l][None, :])

    return _ffn(x, o_mla, o_ret, gt1, sh2, sc2, gt2, g_norm2[l][None, :], g_final[None, :],
                w_out[l].astype(BF16), wup, cw, cb, wdn)
```

```python
import functools

import jax
import jax.numpy as jnp
import numpy as np
from jax import lax
from jax.experimental import pallas as pl
from jax.experimental.pallas import tpu as pltpu

F32 = jnp.float32
BF16 = jnp.bfloat16

GRID_W = 64
MLA_HEADS = 8
MLA_NOPE = 64
MLA_ROPE = 32
MLA_V = 64
MLA_SCALE = (MLA_NOPE + MLA_ROPE) ** -0.5
Q_SCALE = MLA_SCALE * float(np.log2(np.e))
SCORE_FLOOR = -1e30
RET_HEADS = 4
RET_DK = 128
RET_DV = 128
ROPE_BASE = 10000.0
EPS = 1e-6

LANES = 128
MXU_DIM = 256
BF16_ROWS = 16
VT_ROWS = MLA_V + BF16_ROWS
VMEM_LIMIT_BYTES = 56 * 1024 * 1024

HEAD_PAD = LANES
PRE_TM = 512
PRE_SUBTILES = 2
ATT_TQ = 512
ATT_SUBTILES = 2
ATT_KB = MXU_DIM
ATT_PV_LAG = 4
RET_CHUNK = 256
FFN_TM = 512
FFN_FC = 256
FFN_SUBTILES = 4
HALO = 8

NT_DIMS = (((1,), (1,)), ((), ()))
TN_DIMS = (((0,), (0,)), ((), ()))


def _dot(a, b):
    return jnp.dot(a, b, preferred_element_type=F32)


def _rmsnorm(x, g):
    return x * lax.rsqrt(jnp.mean(x * x, axis=-1, keepdims=True) + EPS) * g


def _silu(x):
    return x * (1.0 / (1.0 + jnp.exp(-x)))


def _params(*semantics):
    return pltpu.CompilerParams(dimension_semantics=semantics, vmem_limit_bytes=VMEM_LIMIT_BYTES)


def _const_spec(shape):
    zeros = (0,) * len(shape)
    return pl.BlockSpec(shape, lambda *_: zeros, pipeline_mode=pl.Buffered(1))


def _adaln_kernel(c_ref, w_ref, b_ref, o_ref):
    a = _silu(c_ref[...])
    o_ref[...] = _dot(a.astype(BF16), w_ref[0].astype(BF16)) + b_ref[...]


def _adaln(cs, w, b, layer):
    rows, d = cs.shape
    n = w.shape[2]
    tn = 1024
    return pl.pallas_call(
        _adaln_kernel,
        out_shape=jax.ShapeDtypeStruct((rows, n), F32),
        grid=(n // tn,),
        in_specs=[pl.BlockSpec((rows, d), lambda j: (0, 0)),
                  pl.BlockSpec((1, d, tn), lambda j: (layer, 0, j)),
                  pl.BlockSpec((1, tn), lambda j: (0, j))],
        out_specs=pl.BlockSpec((rows, tn), lambda j: (0, j)),
        compiler_params=_params("parallel"),
        name="adaln",
    )(cs, w, b)


COL_CQ = 0
COL_CKV = 256
COL_KPE = 512
COL_KPE_ROT = 640
COL_RQ = 768
COL_RK = 1280
COL_RV = 1792
COL_RG = 2304
COLS_IN = 2816
RET_W = RET_HEADS * RET_DK


def _preproj_kernel(with_queries, x_ref, sh_ref, sc_ref, g1_ref, win_ref, gq_ref, wuq_ref,
                    gkv_ref, wk_ref, wv_ref, cm_ref, sm_ref, cmt_ref, smt_ref, cr_ref, sr_ref, *out_refs):
    if with_queries:
        q_ref, k_ref, v_ref, rq_ref, rk_ref, rv_ref, sg_ref = out_refs
    else:
        k_ref, v_ref, rk_ref, rv_ref = out_refs
    n_sub = max(1, min(PRE_SUBTILES, x_ref.shape[1] // MXU_DIM))
    rp = x_ref.shape[1] // n_sub
    gain = g1_ref[...] * (1.0 + sc_ref[0])
    shift = sh_ref[0]

    projs = []
    for r in range(n_sub):
        rows = slice(r * rp, (r + 1) * rp)
        h = _rmsnorm(x_ref[0, rows, :], gain) + shift
        projs.append(_dot(h.astype(BF16), win_ref[...]))

    for r in range(n_sub):
        rows = slice(r * rp, (r + 1) * rp)
        proj = projs[r]
        cm = cm_ref[rows, :]
        sm = sm_ref[rows, :]
        cr = cr_ref[rows, :]
        sr = sr_ref[rows, :]

        ckvn = _rmsnorm(proj[:, COL_CKV:COL_CKV + 256], gkv_ref[...]).astype(BF16)
        kn = _dot(ckvn, wk_ref[...])
        vt = lax.dot_general(wv_ref[...], ckvn, NT_DIMS, preferred_element_type=F32).astype(BF16)
        ones_rows = jnp.ones((BF16_ROWS, rp), BF16)
        for hd in range(MLA_HEADS):
            v_ref[0, hd * VT_ROWS:hd * VT_ROWS + MLA_V, rows] = vt[hd * MLA_V:(hd + 1) * MLA_V]
            v_ref[0, hd * VT_ROWS + MLA_V:(hd + 1) * VT_ROWS, rows] = ones_rows
        kpe = proj[:, COL_KPE:COL_KPE + LANES] * cm + proj[:, COL_KPE_ROT:COL_KPE_ROT + LANES] * sm
        for hd in range(MLA_HEADS):
            sl = slice(hd * HEAD_PAD, (hd + 1) * HEAD_PAD)
            k_ref[0, rows, sl] = (kn[:, sl] + kpe).astype(BF16)

        for hd in range(RET_HEADS):
            sl = slice(hd * RET_DK, (hd + 1) * RET_DK)
            b = proj[:, COL_RK + hd * RET_DK:COL_RK + (hd + 1) * RET_DK]
            rk_ref[0, rows, sl] = ((b * cr + pltpu.roll(b, RET_DK // 2, 1) * sr) * (RET_DK ** -0.5)).astype(BF16)
        rv_ref[0, rows, :] = proj[:, COL_RV:COL_RV + RET_W].astype(BF16)

        if with_queries:
            cqn = _rmsnorm(proj[:, COL_CQ:COL_CQ + 256], gq_ref[...]).astype(BF16)
            qa = lax.dot_general(wuq_ref[...], cqn, NT_DIMS, preferred_element_type=F32)
            cmt = cmt_ref[:, rows]
            smt = smt_ref[:, rows]
            rot0 = MLA_HEADS * HEAD_PAD
            for hd in range(MLA_HEADS):
                sl = slice(hd * HEAD_PAD, (hd + 1) * HEAD_PAD)
                a = qa[sl, :]
                b = qa[rot0 + hd * HEAD_PAD:rot0 + (hd + 1) * HEAD_PAD, :]
                q_ref[0, sl, rows] = ((a * cmt + b * smt) * Q_SCALE).astype(BF16)
            for hd in range(RET_HEADS):
                sl = slice(hd * RET_DK, (hd + 1) * RET_DK)
                a = proj[:, COL_RQ + hd * RET_DK:COL_RQ + (hd + 1) * RET_DK]
                rq_ref[0, rows, sl] = (a * cr + pltpu.roll(a, RET_DK // 2, 1) * sr).astype(BF16)
            sg_ref[0, rows, :] = _silu(proj[:, COL_RG:COL_RG + RET_W]).astype(BF16)


def _preproj(x, sh, sc, g1, win, gq, wuq, gkv, wk, wv, cm, sm, cr, sr, *, with_queries):
    bsz, seq, d = x.shape
    tm = min(PRE_TM, seq)
    qk_w = MLA_HEADS * HEAD_PAD
    v_w = MLA_HEADS * VT_ROWS
    cmt, smt = cm.T, sm.T
    tabt_spec = pl.BlockSpec((LANES, tm), lambda b, i: (0, i))
    qt_shape = jax.ShapeDtypeStruct((bsz, qk_w, seq), BF16)
    qt_spec = pl.BlockSpec((1, qk_w, tm), lambda b, i: (b, 0, i))

    def tok(width):
        return pl.BlockSpec((1, tm, width), lambda b, i: (b, i, 0))

    def out(width):
        return jax.ShapeDtypeStruct((bsz, seq, width), BF16)

    mod_spec = pl.BlockSpec((1, 1, d), lambda b, i: (b, 0, 0))
    tab_spec = pl.BlockSpec((tm, LANES), lambda b, i: (i, 0))
    vt_shape = jax.ShapeDtypeStruct((bsz, v_w, seq), BF16)
    vt_spec = pl.BlockSpec((1, v_w, tm), lambda b, i: (b, 0, i))
    if with_queries:
        out_shape = [qt_shape, out(qk_w), vt_shape, out(RET_W), out(RET_W), out(RET_W), out(RET_W)]
        out_specs = [qt_spec, tok(qk_w), vt_spec, tok(RET_W), tok(RET_W), tok(RET_W), tok(RET_W)]
    else:
        out_shape = [out(qk_w), vt_shape, out(RET_W), out(RET_W)]
        out_specs = [tok(qk_w), vt_spec, tok(RET_W), tok(RET_W)]
    return pl.pallas_call(
        functools.partial(_preproj_kernel, with_queries),
        out_shape=out_shape,
        grid=(bsz, seq // tm),
        in_specs=[tok(d), mod_spec, mod_spec, _const_spec(g1.shape), _const_spec(win.shape),
                  _const_spec(gq.shape), _const_spec(wuq.shape), _const_spec(gkv.shape),
                  _const_spec(wk.shape), _const_spec(wv.shape), tab_spec, tab_spec, tabt_spec, tabt_spec,
                  tab_spec, tab_spec],
        out_specs=out_specs,
        compiler_params=_params("parallel", "parallel"),
        name="preproj_q" if with_queries else "preproj_ctx",
    )(x, sh, sc, g1, win, gq, wuq, gkv, wk, wv, cm, sm, cmt, smt, cr, sr)


def _mla_attn_kernel(q_ref, kl_ref, kc_ref, vl_ref, vc_ref, o_ref, s_ref):
    lat_len, ctx_len = kl_ref.shape[1], kc_ref.shape[1]
    kb_ctx = min(ATT_KB, ctx_len)
    blocks = ([(False, k0, ATT_KB) for k0 in range(0, lat_len, ATT_KB)]
              + [(True, k0, kb_ctx) for k0 in range(0, ctx_len, kb_ctx)])
    n_blk = len(blocks)
    pair_rows = 2 * MLA_V

    def s_rows(b):
        from_ctx, k0, size = blocks[b]
        k0 += lat_len if from_ctx else 0
        return slice(k0, k0 + size)

    def k_block(h, b):
        from_ctx, k0, size = blocks[b]
        ref = kc_ref if from_ctx else kl_ref
        return ref[0, k0:k0 + size, h * HEAD_PAD:(h + 1) * HEAD_PAD]

    def vt_block(h, b):
        from_ctx, k0, size = blocks[b]
        ref = vc_ref if from_ctx else vl_ref
        return ref[0, h * VT_ROWS:(h + 1) * VT_ROWS, k0:k0 + size]

    tq = s_ref.shape[2]
    n_rows = (q_ref.shape[2] // tq) * MLA_HEADS
    n_units = n_rows * n_blk
    exp_lag = n_blk
    pv_lag = exp_lag + ATT_PV_LAG
    m_run = [None] * n_rows
    ot = [None] * n_rows
    p_blocks = {}
    for t in range(n_units + pv_lag):
        bm = None
        if t < n_units:
            r, b = divmod(t, n_blk)
            qi, h = divmod(r, MLA_HEADS)
            qt = q_ref[0, h * HEAD_PAD:(h + 1) * HEAD_PAD, qi * tq:(qi + 1) * tq]
            s = _dot(k_block(h, b), qt)
            s_ref[r % 2, s_rows(b), :] = s
            bm = jnp.max(s, axis=0, keepdims=True)
            m_run[r] = bm if m_run[r] is None else jnp.maximum(m_run[r], bm)
        e = t - exp_lag
        if 0 <= e < n_units:
            r, b = divmod(e, n_blk)
            m_use = m_run[r]
            if bm is not None:
                m_use = jnp.maximum(m_use, jnp.minimum(bm, SCORE_FLOOR))
            p_blocks[e] = jnp.exp2(s_ref[r % 2, s_rows(b), :] - m_use).astype(BF16)
        v = t - pv_lag
        if 0 <= v < n_units:
            r, b = divmod(v, n_blk)
            qi, h = divmod(r, MLA_HEADS)
            pv = _dot(vt_block(h, b), p_blocks.pop(v))
            ot[r] = pv if ot[r] is None else ot[r] + pv
            if b == n_blk - 1:
                o_h = ot[r][:MLA_V] * (1.0 / ot[r][MLA_V:MLA_V + 1])
                ot[r] = o_h
                if h % 2 == 1:
                    pair_t = jnp.concatenate([ot[r - 1], o_h], axis=0)
                    cols = slice((h // 2) * pair_rows, (h // 2 + 1) * pair_rows)
                    o_ref[0, qi * tq:(qi + 1) * tq, cols] = pair_t.T.astype(BF16)
                    ot[r - 1] = ot[r] = None


def _mla_attn(qt, k_lat, k_ctx, vt_lat, vt_ctx):
    bsz, qk_w, seq = qt.shape
    ctx_len = k_ctx.shape[1]
    v_w = vt_lat.shape[1]
    o_w = MLA_HEADS * MLA_V
    tq = min(ATT_TQ, seq)
    tstep = min(ATT_TQ * ATT_SUBTILES, seq)
    assert seq % ATT_KB == 0 and ctx_len % min(ATT_KB, ctx_len) == 0 and tstep % tq == 0
    return pl.pallas_call(
        _mla_attn_kernel,
        out_shape=jax.ShapeDtypeStruct((bsz, seq, o_w), BF16),
        grid=(bsz, seq // tstep),
        in_specs=[pl.BlockSpec((1, qk_w, tstep), lambda b, i: (b, 0, i)),
                  pl.BlockSpec((1, seq, qk_w), lambda b, i: (b, 0, 0)),
                  pl.BlockSpec((1, ctx_len, qk_w), lambda b, i: (b, 0, 0)),
                  pl.BlockSpec((1, v_w, seq), lambda b, i: (b, 0, 0)),
                  pl.BlockSpec((1, v_w, ctx_len), lambda b, i: (b, 0, 0))],
        out_specs=pl.BlockSpec((1, tstep, o_w), lambda b, i: (b, i, 0)),
        scratch_shapes=[pltpu.VMEM((2, seq + ctx_len, tq), F32)],
        compiler_params=_params("parallel", "parallel"),
        name="mla_attn",
    )(qt, k_lat, k_ctx, vt_lat, vt_ctx)


def _retention_kernel(rd_ref, q_ref, k_ref, v_ref, sg_ref, kc_ref, vc_ref, g_ref, o_ref, pf_ref, sb_ref):
    c = RET_CHUNK
    n_chunks = q_ref.shape[1] // c
    row = lax.broadcasted_iota(jnp.int32, (c, RET_DV), 0).astype(F32)
    diff = (lax.broadcasted_iota(jnp.int32, (c, c), 0) - lax.broadcasted_iota(jnp.int32, (c, c), 1)).astype(F32)

    for hd in range(RET_HEADS):
        hs = slice(hd * RET_DK, (hd + 1) * RET_DK)
        lgf = -jnp.exp(rd_ref[hd, 0:1, :])
        lgb = -jnp.exp(rd_ref[hd, 1:2, :])
        lgf1 = lgf[:, :RET_DV]
        lgb1 = lgb[:, :RET_DV]
        zeta_f = jnp.exp(lgf1 * (c - 1.0 - row))
        zeta_b = jnp.exp(lgb1 * row)
        xi_f = jnp.exp(lgf1 * (row + 1.0))
        xi_b = jnp.exp(lgb1 * (c - row))
        gc_f = jnp.exp(lgf1 * float(c))
        gc_b = jnp.exp(lgb1 * float(c))
        dmat = jnp.where(diff >= 0.0, jnp.exp(lgf * jnp.maximum(diff, 0.0)), jnp.exp(lgb * jnp.maximum(-diff, 0.0)))

        def kv_outer(kk, vv, zeta_f=zeta_f, zeta_b=zeta_b):
            vf = vv.astype(F32)
            v2 = jnp.concatenate([vf * zeta_f, vf * zeta_b], axis=1).astype(BF16)
            return lax.dot_general(kk, v2, TN_DIMS, preferred_element_type=F32)

        p_ctx = kv_outer(kc_ref[0, :, hs], vc_ref[0, :, hs])
        sf = p_ctx[:, :RET_DV]
        sb = p_ctx[:, RET_DV:]

        for ci in reversed(range(n_chunks)):
            rows = slice(ci * c, (ci + 1) * c)
            p = kv_outer(k_ref[0, rows, hs], v_ref[0, rows, hs])
            pf_ref[hd * n_chunks + ci] = p[:, :RET_DV]
            sb_ref[hd * n_chunks + ci] = sb
            sb = gc_b * sb + p[:, RET_DV:]

        g = g_ref[:, hs]
        for ci in range(n_chunks):
            rows = slice(ci * c, (ci + 1) * c)
            qc = q_ref[0, rows, hs]
            s = lax.dot_general(qc, k_ref[0, rows, hs], NT_DIMS, preferred_element_type=F32)
            inner = _dot((s * dmat).astype(BF16), v_ref[0, rows, hs])
            st = jnp.concatenate([sf, sb_ref[hd * n_chunks + ci]], axis=1).astype(BF16)
            cross = _dot(qc, st)
            y = inner + cross[:, :RET_DV] * xi_f + cross[:, RET_DV:] * xi_b
            mu = jnp.mean(y, axis=-1, keepdims=True)
            d = y - mu
            var = jnp.mean(d * d, axis=-1, keepdims=True)
            yn = d * lax.rsqrt(var + EPS) * g
            o_ref[0, rows, hs] = (sg_ref[0, rows, hs].astype(F32) * yn).astype(BF16)
            sf = gc_f * sf + pf_ref[hd * n_chunks + ci]


def _retention(rd, rq, rk, rv, sg, rk_ctx, rv_ctx, g_ret):
    bsz, seq, width = rq.shape
    ctx_len = rk_ctx.shape[1]
    assert ctx_len == RET_CHUNK and seq % RET_CHUNK == 0

    def tokens(rows):
        return pl.BlockSpec((1, rows, width), lambda b: (b, 0, 0))

    n_states = RET_HEADS * (seq // RET_CHUNK)
    return pl.pallas_call(
        _retention_kernel,
        out_shape=jax.ShapeDtypeStruct((bsz, seq, width), BF16),
        grid=(bsz,),
        in_specs=[_const_spec(rd.shape), tokens(seq), tokens(seq), tokens(seq), tokens(seq),
                  tokens(ctx_len), tokens(ctx_len), _const_spec(g_ret.shape)],
        out_specs=tokens(seq),
        scratch_shapes=[pltpu.VMEM((n_states, RET_DK, RET_DV), F32), pltpu.VMEM((n_states, RET_DK, RET_DV), F32)],
        compiler_params=_params("parallel"),
        name="retention",
    )(rd, rq, rk, rv, sg, rk_ctx, rv_ctx, g_ret)


def _ffn_kernel(x_ref, xp_ref, xn_ref, om_ref, omp_ref, omn_ref, or_ref, orp_ref, orn_ref,
                gt1_ref, sh_ref, sc_ref, gt_ref, g2_ref, gf_ref, wo_ref,
                wup_ref, cw_ref, cb_ref, wdn_ref, o_ref, hs_ref, os_ref, acc_ref):
    tm = x_ref.shape[1]
    n_inner = wup_ref.shape[0]
    half = om_ref.shape[2]
    i = pl.program_id(1)

    def edge(before_ref, after_ref):
        return jnp.concatenate([before_ref[0].astype(F32)[BF16_ROWS - HALO:], after_ref[0].astype(F32)[:HALO]],
                               axis=0).astype(BF16)

    gt1 = gt1_ref[0]
    gain = g2_ref[...] * (1.0 + sc_ref[0])
    shift = sh_ref[0]
    has_prev = (i > 0).astype(F32)
    has_next = (i < pl.num_programs(1) - 1).astype(F32)
    row8 = lax.broadcasted_iota(jnp.int32, (HALO, 2 * FFN_FC), 0)

    n_sub = FFN_SUBTILES
    rp = tm // n_sub
    last = n_sub - 1

    def up_proj(j, r):
        rows = slice(r * rp, (r + 1) * rp + (2 * HALO if r == last else 0))
        return _dot(hs_ref[rows, :], wup_ref[j])

    def conv_act(u, j, r):
        um = u[r][0:rp]
        if r == 0:
            u_before = u[last][rp + HALO - 1:rp + HALO] * has_prev
        else:
            u_before = u[r - 1][rp - 1:rp]
        if r == last:
            u_after = u[last][rp + HALO:rp + HALO + 1] * has_next
        else:
            u_after = u[r + 1][0:1]
        up = pltpu.roll(um, 1, 0)
        up = jnp.concatenate([jnp.where(row8 == 0, u_before, up[0:HALO]), up[HALO:]], axis=0)
        un = pltpu.roll(um, rp - 1, 0)
        un = jnp.concatenate([un[:rp - HALO], jnp.where(row8 == HALO - 1, u_after, un[rp - HALO:])], axis=0)
        cw = cw_ref[j]
        cv = up * cw[0:1] + um * cw[1:2] + un * cw[2:3] + cb_ref[j]
        return (_silu(cv[:, FFN_FC:]) * cv[:, :FFN_FC]).astype(BF16)

    os_ref[0:tm, 0:half] = om_ref[0]
    os_ref[0:tm, half:] = or_ref[0]
    os_ref[tm:, 0:half] = edge(omp_ref, omn_ref)
    os_ref[tm:, half:] = edge(orp_ref, orn_ref)
    mixed = _dot(os_ref[...], wo_ref[...])
    x = x_ref[0] + gt1 * mixed[0:tm]
    o_ref[0] = x
    halo = jnp.concatenate([xp_ref[0], xn_ref[0]], axis=0) + gt1 * mixed[tm:]
    hs_ref[tm:tm + 2 * HALO, :] = (_rmsnorm(halo, gain) + shift).astype(BF16)
    u = []
    for r in range(n_sub):
        rows = slice(r * rp, (r + 1) * rp)
        hs_ref[rows, :] = (_rmsnorm(x[rows], gain) + shift).astype(BF16)
        u.append(up_proj(0, r))

    pending = [None] * n_sub
    acc_live = [False] * n_sub
    for j in range(n_inner):
        u_next = [None] * n_sub
        for r in range(n_sub):
            rows = slice(r * rp, (r + 1) * rp)
            if j + 1 < n_inner:
                u_next[r] = up_proj(j + 1, r)
            part = _dot(conv_act(u, j, r), wdn_ref[j])
            if pending[r] is None and j + 1 < n_inner:
                pending[r] = part
                continue
            total = part if pending[r] is None else pending[r] + part
            pending[r] = None
            if acc_live[r]:
                total = acc_ref[rows, :] + total
            if j + 1 < n_inner:
                acc_ref[rows, :] = total
                acc_live[r] = True
            else:
                o_ref[0, rows, :] = _rmsnorm(o_ref[0, rows, :] + gt_ref[0] * total, gf_ref[...])
        u = u_next


def _ffn(x, o_mla, o_ret, gt1, sh, sc, gt, g2, gf, wo, wup, cw, cb, wdn):
    bsz, seq, d = x.shape
    tm = min(FFN_TM, seq)
    mod_spec = pl.BlockSpec((1, 1, d), lambda b, i: (b, 0, 0))

    def tile(width):
        return pl.BlockSpec((1, tm, width), lambda b, i: (b, i, 0))

    def before(rows, width):
        return pl.BlockSpec((1, rows, width), lambda b, i: (b, jnp.maximum(i * (tm // rows) - 1, 0), 0))

    def after(rows, width):
        last = seq // rows - 1
        return pl.BlockSpec((1, rows, width), lambda b, i: (b, jnp.minimum((i + 1) * (tm // rows), last), 0))

    mw, rw = o_mla.shape[2], o_ret.shape[2]
    return pl.pallas_call(
        _ffn_kernel,
        out_shape=jax.ShapeDtypeStruct((bsz, seq, d), F32),
        grid=(bsz, seq // tm),
        in_specs=[tile(d), before(HALO, d), after(HALO, d),
                  tile(mw), before(BF16_ROWS, mw), after(BF16_ROWS, mw),
                  tile(rw), before(BF16_ROWS, rw), after(BF16_ROWS, rw),
                  mod_spec, mod_spec, mod_spec, mod_spec, _const_spec(g2.shape), _const_spec(gf.shape),
                  _const_spec(wo.shape),
                  _const_spec(wup.shape), _const_spec(cw.shape), _const_spec(cb.shape), _const_spec(wdn.shape)],
        out_specs=tile(d),
        scratch_shapes=[pltpu.VMEM((tm + 2 * HALO, d), BF16), pltpu.VMEM((tm + 2 * HALO, mw + rw), BF16),
                        pltpu.VMEM((tm, d), F32)],
        compiler_params=_params("parallel", "parallel"),
        name="ffn",
    )(x, x, x, o_mla, o_mla, o_mla, o_ret, o_ret, o_ret, gt1, sh, sc, gt, g2, gf, wo, wup, cw, cb, wdn)


def _rot_half_cols(w, half):
    shape = w.shape
    w = w.reshape(shape[:-1] + (shape[-1] // (2 * half), 2, half))
    return jnp.stack([-w[..., 1, :], w[..., 0, :]], axis=-2).reshape(shape)


def _rope_angles(pos, dim):
    inv = ROPE_BASE ** (-np.arange(0, dim, 2, dtype=np.float64) / dim)
    return pos.astype(np.float64)[:, None] * inv[None, :]


def _mla_tables(seq, with_pos):
    if not with_pos:
        return np.ones((seq, LANES), np.float32), np.zeros((seq, LANES), np.float32)
    pos = np.arange(seq)
    quarter = MLA_ROPE // 2
    ar = _rope_angles(pos // GRID_W, quarter)
    ac = _rope_angles(pos % GRID_W, quarter)
    ang = np.concatenate([ar, ar, ac, ac], axis=-1)
    pad = LANES - MLA_NOPE - MLA_ROPE
    cos = np.concatenate([np.ones((seq, MLA_NOPE)), np.cos(ang), np.ones((seq, pad))], axis=-1)
    sin = np.concatenate([np.zeros((seq, MLA_NOPE)), np.sin(ang), np.zeros((seq, pad))], axis=-1)
    return cos.astype(np.float32), sin.astype(np.float32)


def _ret_tables(seq, with_pos):
    if not with_pos:
        return np.ones((seq, RET_DK), np.float32), np.zeros((seq, RET_DK), np.float32)
    ang = _rope_angles(np.arange(seq), RET_DK)
    return (np.concatenate([np.cos(ang), np.cos(ang)], axis=-1).astype(np.float32),
            np.concatenate([-np.sin(ang), np.sin(ang)], axis=-1).astype(np.float32))


def _pad_heads(nope, rope):
    r = nope.shape[0]
    pad = jnp.zeros((r, MLA_HEADS, HEAD_PAD - MLA_NOPE - MLA_ROPE), nope.dtype)
    return jnp.concatenate([nope, rope, pad], axis=-1).reshape(r, MLA_HEADS * HEAD_PAD)


def _pair_up_kernel(a_ref, g_ref, o_ref):
    o_ref[0, :, :FFN_FC] = a_ref[0].astype(BF16)
    o_ref[0, :, FFN_FC:] = g_ref[0].astype(BF16)


def _pair_up_weights(w_up, layer, nj):
    d = w_up.shape[1]
    return pl.pallas_call(
        _pair_up_kernel,
        out_shape=jax.ShapeDtypeStruct((nj, d, 2 * FFN_FC), BF16),
        grid=(nj,),
        in_specs=[pl.BlockSpec((1, d, FFN_FC), lambda j: (layer, 0, j)),
                  pl.BlockSpec((1, d, FFN_FC), lambda j: (layer, 0, nj + j))],
        out_specs=pl.BlockSpec((1, d, 2 * FFN_FC), lambda j: (j, 0, 0)),
        compiler_params=_params("parallel"),
        name="pair_up_weights",
    )(w_up, w_up)


def _rope_placement():
    place = np.zeros((LANES, LANES), np.float32)
    rot = np.zeros((LANES, LANES), np.float32)
    half = MLA_ROPE // 4
    for c in range(MLA_ROPE):
        place[c, MLA_NOPE + c] = 1.0
        g, w = divmod(c, 2 * half)
        if w < half:
            rot[g * 2 * half + w + half, MLA_NOPE + c] = -1.0
        else:
            rot[g * 2 * half + w - half, MLA_NOPE + c] = 1.0
    return place, rot


def _win_kernel(w_ref, place_ref, rot_ref, o_ref):
    w = w_ref[0]
    o_ref[:, 0:COL_KPE] = w[:, 0:COL_KPE].astype(BF16)
    blk = w[:, COL_KPE:COL_KPE + LANES].astype(BF16)
    o_ref[:, COL_KPE:COL_KPE + LANES] = _dot(blk, place_ref[...]).astype(BF16)
    o_ref[:, COL_KPE_ROT:COL_KPE_ROT + LANES] = _dot(blk, rot_ref[...]).astype(BF16)
    o_ref[:, COL_RQ:] = w[:, COL_KPE + MLA_ROPE:].astype(BF16)


def _layout_w_in(w_in, layer):
    d, n = w_in.shape[1], w_in.shape[2]
    assert n - MLA_ROPE + 2 * LANES == COLS_IN
    rows = 256
    place, rot = _rope_placement()
    return pl.pallas_call(
        _win_kernel,
        out_shape=jax.ShapeDtypeStruct((d, COLS_IN), BF16),
        grid=(d // rows,),
        in_specs=[pl.BlockSpec((1, rows, n), lambda i: (layer, i, 0)),
                  _const_spec(place.shape), _const_spec(rot.shape)],
        out_specs=pl.BlockSpec((rows, COLS_IN), lambda i: (i, 0)),
        compiler_params=_params("parallel"),
        name="layout_w_in",
    )(w_in, place.astype(BF16), rot.astype(BF16))


def _layout_weights(w_in_stack, w_uq, w_ukv, w_up_stack, layer, conv_w, conv_b, w_down):
    q_rank = w_uq.shape[0]
    rot_half = MLA_ROPE // 4
    win = _layout_w_in(w_in_stack, layer)

    uq = w_uq.reshape(q_rank, MLA_HEADS, MLA_NOPE + MLA_ROPE)
    uq_n, uq_r = uq[..., :MLA_NOPE], uq[..., MLA_NOPE:]
    wuq = jnp.concatenate([_pad_heads(uq_n, uq_r),
                           _pad_heads(jnp.zeros_like(uq_n), _rot_half_cols(uq_r, rot_half))],
                          axis=-1).T.astype(BF16)

    ukv =w_ukv.reshape(w_ukv.shape[0], MLA_HEADS, MLA_NOPE + MLA_V)
    wk = _pad_heads(ukv[..., :MLA_NOPE], jnp.zeros(ukv.shape[:2] + (MLA_ROPE,), ukv.dtype)).astype(BF16)
    wv = ukv[..., MLA_NOPE:].reshape(w_ukv.shape[0], MLA_HEADS * MLA_V).T.astype(BF16)

    d_ff = w_down.shape[0]
    nj = d_ff // FFN_FC

    def pair_chunks(a):
        lead = a.shape[:-1]
        a = a.reshape(lead + (2, nj, FFN_FC))
        a = jnp.moveaxis(a, -2, 0)
        return a.reshape((nj,) + lead + (2 * FFN_FC,))

    wup = _pair_up_weights(w_up_stack, layer, nj)
    cw = pair_chunks(conv_w)
    cb = pair_chunks(conv_b[None, :])
    wdn = w_down.reshape(nj, FFN_FC, w_down.shape[1]).astype(BF16)
    return win, wuq, wk, wv, wup, cw, cb, wdn


def kernel(x, c, ctx, c_ctx, w_ada, b_ada, g_norm1, w_in, g_q, w_uq, g_kv, w_ukv, ret_decay, g_ret,
           w_out, g_norm2, w_up, conv_w, conv_b, w_down, g_final):
    bsz, seq, d = x.shape
    ctx_len = ctx.shape[1]
    depth = w_ada.shape[0]
    assert depth == 1, "single-layer block"
    l = 0

    rows = -(-(bsz + 1) // 8) * 8
    cs = jnp.concatenate([c, c_ctx[None, :], jnp.zeros((rows - bsz - 1, d), F32)], axis=0)
    mod = _adaln(cs, w_ada, b_ada[l][None, :], l)
    sh1, sc1, gt1, sh2, sc2, gt2 = [mod[:bsz, t * d:(t + 1) * d][:, None, :] for t in range(6)]
    shc1 = jnp.broadcast_to(mod[bsz, 0:d][None, None, :], (bsz, 1, d))
    scc1 = jnp.broadcast_to(mod[bsz, d:2 * d][None, None, :], (bsz, 1, d))

    win, wuq, wk, wv, wup, cw, cb, wdn = _layout_weights(w_in, w_uq[l], w_ukv[l], w_up, l, conv_w[l],
                                                       conv_b[l], w_down[l])
    g1 = g_norm1[l][None, :]
    gq = g_q[l][None, :]
    gkv = g_kv[l][None, :]

    cm, sm = _mla_tables(seq, True)
    cr, sr = _ret_tables(seq, True)
    q, k, v, rq, rk, rv, sg = _preproj(x, sh1, sc1, g1, win, gq, wuq, gkv, wk, wv, cm, sm, cr, sr,
                                       with_queries=True)
    cm0, sm0 = _mla_tables(ctx_len, False)
    cr0, sr0 = _ret_tables(ctx_len, False)
    k_c, v_c, rk_c, rv_c = _preproj(ctx, shc1, scc1, g1, win, gq, wuq, gkv, wk, wv, cm0, sm0, cr0, sr0,
                                    with_queries=False)

    o_mla = _mla_attn(q, k, k_c, v, v_c)
    rd = jnp.broadcast_to(jnp.transpose(ret_decay[l])[:, :, None], (RET_HEADS, 2, RET_CHUNK)).astype(F32)
    o_ret = _retention(rd, rq, rk, rv, sg, rk_c, rv_c, g_ret[l][None, :])

    return _ffn(x, o_mla, o_ret, gt1, sh2, sc2, gt2, g_norm2[l][None, :], g_final[None, :],
                w_out[l].astype(BF16), wup, cw, cb, wdn)
```

```python
import functools

import jax
import jax.numpy as jnp
import numpy as np
from jax import lax
from jax.experimental import pallas as pl
from jax.experimental.pallas import tpu as pltpu

F32 = jnp.float32
BF16 = jnp.bfloat16

GRID_W = 64
MLA_HEADS = 8
MLA_NOPE = 64
MLA_ROPE = 32
MLA_V = 64
MLA_SCALE = (MLA_NOPE + MLA_ROPE) ** -0.5
Q_SCALE = MLA_SCALE * float(np.log2(np.e))
SCORE_FLOOR = -1e30
RET_HEADS = 4
RET_DK = 128
RET_DV = 128
ROPE_BASE = 10000.0
EPS = 1e-6

LANES = 128
MXU_DIM = 256
BF16_ROWS = 16
VT_ROWS = MLA_V + BF16_ROWS
VMEM_LIMIT_BYTES = 56 * 1024 * 1024

HEAD_PAD = LANES
PRE_TM = 512
PRE_SUBTILES = 2
ATT_TQ = 512
ATT_SUBTILES = 2
ATT_KB = MXU_DIM
ATT_PV_LAG = 2
RET_CHUNK = 256
FFN_TM = 512
FFN_FC = 256
FFN_SUBTILES = 4
HALO = 8

NT_DIMS = (((1,), (1,)), ((), ()))
TN_DIMS = (((0,), (0,)), ((), ()))


def _dot(a, b):
    return jnp.dot(a, b, preferred_element_type=F32)


def _rmsnorm(x, g):
    return x * lax.rsqrt(jnp.mean(x * x, axis=-1, keepdims=True) + EPS) * g


def _silu(x):
    return x * (1.0 / (1.0 + jnp.exp(-x)))


def _params(*semantics):
    return pltpu.CompilerParams(dimension_semantics=semantics, vmem_limit_bytes=VMEM_LIMIT_BYTES)


def _const_spec(shape):
    zeros = (0,) * len(shape)
    return pl.BlockSpec(shape, lambda *_: zeros, pipeline_mode=pl.Buffered(1))


def _adaln_kernel(c_ref, w_ref, b_ref, o_ref):
    a = _silu(c_ref[...])
    o_ref[...] = _dot(a.astype(BF16), w_ref[0].astype(BF16)) + b_ref[...]


def _adaln(cs, w, b, layer):
    rows, d = cs.shape
    n = w.shape[2]
    tn = 1024
    return pl.pallas_call(
        _adaln_kernel,
        out_shape=jax.ShapeDtypeStruct((rows, n), F32),
        grid=(n // tn,),
        in_specs=[pl.BlockSpec((rows, d), lambda j: (0, 0)),
                  pl.BlockSpec((1, d, tn), lambda j: (layer, 0, j)),
                  pl.BlockSpec((1, tn), lambda j: (0, j))],
        out_specs=pl.BlockSpec((rows, tn), lambda j: (0, j)),
        compiler_params=_params("parallel"),
        name="adaln",
    )(cs, w, b)


COL_CQ = 0
COL_CKV = 256
COL_KPE = 512
COL_KPE_ROT = 640
COL_RQ = 768
COL_RK = 1280
COL_RV = 1792
COL_RG = 2304
COLS_IN = 2816
RET_W = RET_HEADS * RET_DK


def _preproj_kernel(with_queries, x_ref, sh_ref, sc_ref, g1_ref, win_ref, gq_ref, wuq_ref,
                    gkv_ref, wk_ref, wv_ref, cm_ref, sm_ref, cmt_ref, smt_ref, cr_ref, sr_ref, *out_refs):
    if with_queries:
        q_ref, k_ref, v_ref, rq_ref, rk_ref, rv_ref, sg_ref = out_refs
    else:
        k_ref, v_ref, rk_ref, rv_ref = out_refs
    n_sub = max(1, min(PRE_SUBTILES, x_ref.shape[1] // MXU_DIM))
    rp = x_ref.shape[1] // n_sub
    gain = g1_ref[...] * (1.0 + sc_ref[0])
    shift = sh_ref[0]

    projs = []
    for r in range(n_sub):
        rows = slice(r * rp, (r + 1) * rp)
        h = _rmsnorm(x_ref[0, rows, :], gain) + shift
        projs.append(_dot(h.astype(BF16), win_ref[...]))

    for r in range(n_sub):
        rows = slice(r * rp, (r + 1) * rp)
        proj = projs[r]
        cm = cm_ref[rows, :]
        sm = sm_ref[rows, :]
        cr = cr_ref[rows, :]
        sr = sr_ref[rows, :]

        ckvn = _rmsnorm(proj[:, COL_CKV:COL_CKV + 256], gkv_ref[...]).astype(BF16)
        kn = _dot(ckvn, wk_ref[...])
        vt = lax.dot_general(wv_ref[...], ckvn, NT_DIMS, preferred_element_type=F32).astype(BF16)
        ones_rows = jnp.ones((BF16_ROWS, rp), BF16)
        for hd in range(MLA_HEADS):
            v_ref[0, hd * VT_ROWS:hd * VT_ROWS + MLA_V, rows] = vt[hd * MLA_V:(hd + 1) * MLA_V]
            v_ref[0, hd * VT_ROWS + MLA_V:(hd + 1) * VT_ROWS, rows] = ones_rows
        kpe = proj[:, COL_KPE:COL_KPE + LANES] * cm + proj[:, COL_KPE_ROT:COL_KPE_ROT + LANES] * sm
        for hd in range(MLA_HEADS):
            sl = slice(hd * HEAD_PAD, (hd + 1) * HEAD_PAD)
            k_ref[0, rows, sl] = (kn[:, sl] + kpe).astype(BF16)

        for hd in range(RET_HEADS):
            sl = slice(hd * RET_DK, (hd + 1) * RET_DK)
            b = proj[:, COL_RK + hd * RET_DK:COL_RK + (hd + 1) * RET_DK]
            rk_ref[0, rows, sl] = ((b * cr + pltpu.roll(b, RET_DK // 2, 1) * sr) * (RET_DK ** -0.5)).astype(BF16)
        rv_ref[0, rows, :] = proj[:, COL_RV:COL_RV + RET_W].astype(BF16)

        if with_queries:
            cqn = _rmsnorm(proj[:, COL_CQ:COL_CQ + 256], gq_ref[...]).astype(BF16)
            qa = lax.dot_general(wuq_ref[...], cqn, NT_DIMS, preferred_element_type=F32)
            cmt = cmt_ref[:, rows]
            smt = smt_ref[:, rows]
            rot0 = MLA_HEADS * HEAD_PAD
            for hd in range(MLA_HEADS):
                sl = slice(hd * HEAD_PAD, (hd + 1) * HEAD_PAD)
                a = qa[sl, :]
                b = qa[rot0 + hd * HEAD_PAD:rot0 + (hd + 1) * HEAD_PAD, :]
                q_ref[0, sl, rows] = ((a * cmt + b * smt) * Q_SCALE).astype(BF16)
            for hd in range(RET_HEADS):
                sl = slice(hd * RET_DK, (hd + 1) * RET_DK)
                a = proj[:, COL_RQ + hd * RET_DK:COL_RQ + (hd + 1) * RET_DK]
                rq_ref[0, rows, sl] = (a * cr + pltpu.roll(a, RET_DK // 2, 1) * sr).astype(BF16)
            sg_ref[0, rows, :] = _silu(proj[:, COL_RG:COL_RG + RET_W]).astype(BF16)


def _preproj(x, sh, sc, g1, win, gq, wuq, gkv, wk, wv, cm, sm, cr, sr, *, with_queries):
    bsz, seq, d = x.shape
    tm = min(PRE_TM, seq)
    qk_w = MLA_HEADS * HEAD_PAD
    v_w = MLA_HEADS * VT_ROWS
    cmt, smt = cm.T, sm.T
    tabt_spec = pl.BlockSpec((LANES, tm), lambda b, i: (0, i))
    qt_shape = jax.ShapeDtypeStruct((bsz, qk_w, seq), BF16)
    qt_spec = pl.BlockSpec((1, qk_w, tm), lambda b, i: (b, 0, i))

    def tok(width):
        return pl.BlockSpec((1, tm, width), lambda b, i: (b, i, 0))

    def out(width):
        return jax.ShapeDtypeStruct((bsz, seq, width), BF16)

    mod_spec = pl.BlockSpec((1, 1, d), lambda b, i: (b, 0, 0))
    tab_spec = pl.BlockSpec((tm, LANES), lambda b, i: (i, 0))
    vt_shape = jax.ShapeDtypeStruct((bsz, v_w, seq), BF16)
    vt_spec = pl.BlockSpec((1, v_w, tm), lambda b, i: (b, 0, i))
    if with_queries:
        out_shape = [qt_shape, out(qk_w), vt_shape, out(RET_W), out(RET_W), out(RET_W), out(RET_W)]
        out_specs = [qt_spec, tok(qk_w), vt_spec, tok(RET_W), tok(RET_W), tok(RET_W), tok(RET_W)]
    else:
        out_shape = [out(qk_w), vt_shape, out(RET_W), out(RET_W)]
        out_specs = [tok(qk_w), vt_spec, tok(RET_W), tok(RET_W)]
    return pl.pallas_call(
        functools.partial(_preproj_kernel, with_queries),
        out_shape=out_shape,
        grid=(bsz, seq // tm),
        in_specs=[tok(d), mod_spec, mod_spec, _const_spec(g1.shape), _const_spec(win.shape),
                  _const_spec(gq.shape), _const_spec(wuq.shape), _const_spec(gkv.shape),
                  _const_spec(wk.shape), _const_spec(wv.shape), tab_spec, tab_spec, tabt_spec, tabt_spec,
                  tab_spec, tab_spec],
        out_specs=out_specs,
        compiler_params=_params("parallel", "parallel"),
        name="preproj_q" if with_queries else "preproj_ctx",
    )(x, sh, sc, g1, win, gq, wuq, gkv, wk, wv, cm, sm, cmt, smt, cr, sr)


def _mla_attn_kernel(q_ref, kl_ref, kc_ref, vl_ref, vc_ref, o_ref, s_ref):
    lat_len, ctx_len = kl_ref.shape[1], kc_ref.shape[1]
    kb_ctx = min(ATT_KB, ctx_len)
    blocks = ([(False, k0, ATT_KB) for k0 in range(0, lat_len, ATT_KB)]
              + [(True, k0, kb_ctx) for k0 in range(0, ctx_len, kb_ctx)])
    n_blk = len(blocks)
    pair_rows = 2 * MLA_V

    def s_rows(b):
        from_ctx, k0, size = blocks[b]
        k0 += lat_len if from_ctx else 0
        return slice(k0, k0 + size)

    def k_block(h, b):
        from_ctx, k0, size = blocks[b]
        ref = kc_ref if from_ctx else kl_ref
        return ref[0, k0:k0 + size, h * HEAD_PAD:(h + 1) * HEAD_PAD]

    def vt_block(h, b):
        from_ctx, k0, size = blocks[b]
        ref = vc_ref if from_ctx else vl_ref
        return ref[0, h * VT_ROWS:(h + 1) * VT_ROWS, k0:k0 + size]

    tq = s_ref.shape[2]
    n_rows = (q_ref.shape[2] // tq) * MLA_HEADS
    n_units = n_rows * n_blk
    exp_lag = n_blk
    pv_lag = exp_lag + ATT_PV_LAG
    m_run = [None] * n_rows
    ot = [None] * n_rows
    p_blocks = {}
    for t in range(n_units + pv_lag):
        bm = None
        if t < n_units:
            r, b = divmod(t, n_blk)
            qi, h = divmod(r, MLA_HEADS)
            qt = q_ref[0, h * HEAD_PAD:(h + 1) * HEAD_PAD, qi * tq:(qi + 1) * tq]
            s = _dot(k_block(h, b), qt)
            s_ref[r % 2, s_rows(b), :] = s
            bm = jnp.max(s, axis=0, keepdims=True)
            m_run[r] = bm if m_run[r] is None else jnp.maximum(m_run[r], bm)
        e = t - exp_lag
        if 0 <= e < n_units:
            r, b = divmod(e, n_blk)
            m_use = m_run[r]
            if bm is not None:
                m_use = jnp.maximum(m_use, jnp.minimum(bm, SCORE_FLOOR))
            p_blocks[e] = jnp.exp2(s_ref[r % 2, s_rows(b), :] - m_use).astype(BF16)
        v = t - pv_lag
        if 0 <= v < n_units:
            r, b = divmod(v, n_blk)
            qi, h = divmod(r, MLA_HEADS)
            pv = _dot(vt_block(h, b), p_blocks.pop(v))
            ot[r] = pv if ot[r] is None else ot[r] + pv
            if b == n_blk - 1:
                o_h = ot[r][:MLA_V] * (1.0 / ot[r][MLA_V:MLA_V + 1])
                ot[r] = o_h
                if h % 2 == 1:
                    pair_t = jnp.concatenate([ot[r - 1], o_h], axis=0)
                    cols = slice((h // 2) * pair_rows, (h // 2 + 1) * pair_rows)
                    o_ref[0, qi * tq:(qi + 1) * tq, cols] = pair_t.T.astype(BF16)
                    ot[r - 1] = ot[r] = None


def _mla_attn(qt, k_lat, k_ctx, vt_lat, vt_ctx):
    bsz, qk_w, seq = qt.shape
    ctx_len = k_ctx.shape[1]
    v_w = vt_lat.shape[1]
    o_w = MLA_HEADS * MLA_V
    tq = min(ATT_TQ, seq)
    tstep = min(ATT_TQ * ATT_SUBTILES, seq)
    assert seq % ATT_KB == 0 and ctx_len % min(ATT_KB, ctx_len) == 0 and tstep % tq == 0
    return pl.pallas_call(
        _mla_attn_kernel,
        out_shape=jax.ShapeDtypeStruct((bsz, seq, o_w), BF16),
        grid=(bsz, seq // tstep),
        in_specs=[pl.BlockSpec((1, qk_w, tstep), lambda b, i: (b, 0, i)),
                  pl.BlockSpec((1, seq, qk_w), lambda b, i: (b, 0, 0)),
                  pl.BlockSpec((1, ctx_len, qk_w), lambda b, i: (b, 0, 0)),
                  pl.BlockSpec((1, v_w, seq), lambda b, i: (b, 0, 0)),
                  pl.BlockSpec((1, v_w, ctx_len), lambda b, i: (b, 0, 0))],
        out_specs=pl.BlockSpec((1, tstep, o_w), lambda b, i: (b, i, 0)),
        scratch_shapes=[pltpu.VMEM((2, seq + ctx_len, tq), F32)],
        compiler_params=_params("parallel", "parallel"),
        name="mla_attn",
    )(qt, k_lat, k_ctx, vt_lat, vt_ctx)


def _retention_kernel(rd_ref, q_ref, k_ref, v_ref, sg_ref, kc_ref, vc_ref, g_ref, o_ref, pf_ref, sb_ref):
    c = RET_CHUNK
    n_chunks = q_ref.shape[1] // c
    row = lax.broadcasted_iota(jnp.int32, (c, RET_DV), 0).astype(F32)
    diff = (lax.broadcasted_iota(jnp.int32, (c, c), 0) - lax.broadcasted_iota(jnp.int32, (c, c), 1)).astype(F32)

    for hd in range(RET_HEADS):
        hs = slice(hd * RET_DK, (hd + 1) * RET_DK)
        lgf = -jnp.exp(rd_ref[hd, 0:1, :])
        lgb = -jnp.exp(rd_ref[hd, 1:2, :])
        lgf1 = lgf[:, :RET_DV]
        lgb1 = lgb[:, :RET_DV]
        zeta_f = jnp.exp(lgf1 * (c - 1.0 - row))
        zeta_b = jnp.exp(lgb1 * row)
        xi_f = jnp.exp(lgf1 * (row + 1.0))
        xi_b = jnp.exp(lgb1 * (c - row))
        gc_f = jnp.exp(lgf1 * float(c))
        gc_b = jnp.exp(lgb1 * float(c))
        dmat = jnp.where(diff >= 0.0, jnp.exp(lgf * jnp.maximum(diff, 0.0)), jnp.exp(lgb * jnp.maximum(-diff, 0.0)))

        def kv_outer(kk, vv, zeta_f=zeta_f, zeta_b=zeta_b):
            vf = vv.astype(F32)
            v2 = jnp.concatenate([vf * zeta_f, vf * zeta_b], axis=1).astype(BF16)
            return lax.dot_general(kk, v2, TN_DIMS, preferred_element_type=F32)

        p_ctx = kv_outer(kc_ref[0, :, hs], vc_ref[0, :, hs])
        sf = p_ctx[:, :RET_DV]
        sb = p_ctx[:, RET_DV:]

        for ci in reversed(range(n_chunks)):
            rows = slice(ci * c, (ci + 1) * c)
            p = kv_outer(k_ref[0, rows, hs], v_ref[0, rows, hs])
            pf_ref[hd * n_chunks + ci] = p[:, :RET_DV]
            sb_ref[hd * n_chunks + ci] = sb
            sb = gc_b * sb + p[:, RET_DV:]

        g = g_ref[:, hs]
        for ci in range(n_chunks):
            rows = slice(ci * c, (ci + 1) * c)
            qc = q_ref[0, rows, hs]
            s = lax.dot_general(qc, k_ref[0, rows, hs], NT_DIMS, preferred_element_type=F32)
            inner = _dot((s * dmat).astype(BF16), v_ref[0, rows, hs])
            st = jnp.concatenate([sf, sb_ref[hd * n_chunks + ci]], axis=1).astype(BF16)
            cross = _dot(qc, st)
            y = inner + cross[:, :RET_DV] * xi_f + cross[:, RET_DV:] * xi_b
            mu = jnp.mean(y, axis=-1, keepdims=True)
            d = y - mu
            var = jnp.mean(d * d, axis=-1, keepdims=True)
            yn = d * lax.rsqrt(var + EPS) * g
            o_ref[0, rows, hs] = (sg_ref[0, rows, hs].astype(F32) * yn).astype(BF16)
            sf = gc_f * sf + pf_ref[hd * n_chunks + ci]


def _retention(rd, rq, rk, rv, sg, rk_ctx, rv_ctx, g_ret):
    bsz, seq, width = rq.shape
    ctx_len = rk_ctx.shape[1]
    assert ctx_len == RET_CHUNK and seq % RET_CHUNK == 0

    def tokens(rows):
        return pl.BlockSpec((1, rows, width), lambda b: (b, 0, 0))

    n_states = RET_HEADS * (seq // RET_CHUNK)
    return pl.pallas_call(
        _retention_kernel,
        out_shape=jax.ShapeDtypeStruct((bsz, seq, width), BF16),
        grid=(bsz,),
        in_specs=[_const_spec(rd.shape), tokens(seq), tokens(seq), tokens(seq), tokens(seq),
                  tokens(ctx_len), tokens(ctx_len), _const_spec(g_ret.shape)],
        out_specs=tokens(seq),
        scratch_shapes=[pltpu.VMEM((n_states, RET_DK, RET_DV), F32), pltpu.VMEM((n_states, RET_DK, RET_DV), F32)],
        compiler_params=_params("parallel"),
        name="retention",
    )(rd, rq, rk, rv, sg, rk_ctx, rv_ctx, g_ret)


def _ffn_kernel(x_ref, xp_ref, xn_ref, om_ref, omp_ref, omn_ref, or_ref, orp_ref, orn_ref,
                gt1_ref, sh_ref, sc_ref, gt_ref, g2_ref, gf_ref, wo_ref,
                wup_ref, cw_ref, cb_ref, wdn_ref, o_ref, hs_ref, os_ref, acc_ref):
    tm = x_ref.shape[1]
    n_inner = wup_ref.shape[0]
    half = om_ref.shape[2]
    i = pl.program_id(1)

    def edge(before_ref, after_ref):
        return jnp.concatenate([before_ref[0].astype(F32)[BF16_ROWS - HALO:], after_ref[0].astype(F32)[:HALO]],
                               axis=0).astype(BF16)

    gt1 = gt1_ref[0]
    gain = g2_ref[...] * (1.0 + sc_ref[0])
    shift = sh_ref[0]
    has_prev = (i > 0).astype(F32)
    has_next = (i < pl.num_programs(1) - 1).astype(F32)
    row8 = lax.broadcasted_iota(jnp.int32, (HALO, 2 * FFN_FC), 0)

    n_sub = FFN_SUBTILES
    rp = tm // n_sub
    last = n_sub - 1

    def up_proj(j, r):
        rows = slice(r * rp, (r + 1) * rp + (2 * HALO if r == last else 0))
        return _dot(hs_ref[rows, :], wup_ref[j])

    def conv_act(u, j, r):
        um = u[r][0:rp]
        if r == 0:
            u_before = u[last][rp + HALO - 1:rp + HALO] * has_prev
        else:
            u_before = u[r - 1][rp - 1:rp]
        if r == last:
            u_after = u[last][rp + HALO:rp + HALO + 1] * has_next
        else:
            u_after = u[r + 1][0:1]
        up = pltpu.roll(um, 1, 0)
        up = jnp.concatenate([jnp.where(row8 == 0, u_before, up[0:HALO]), up[HALO:]], axis=0)
        un = pltpu.roll(um, rp - 1, 0)
        un = jnp.concatenate([un[:rp - HALO], jnp.where(row8 == HALO - 1, u_after, un[rp - HALO:])], axis=0)
        cw = cw_ref[j]
        cv = up * cw[0:1] + um * cw[1:2] + un * cw[2:3] + cb_ref[j]
        return (_silu(cv[:, FFN_FC:]) * cv[:, :FFN_FC]).astype(BF16)

    os_ref[0:tm, 0:half] = om_ref[0]
    os_ref[0:tm, half:] = or_ref[0]
    os_ref[tm:, 0:half] = edge(omp_ref, omn_ref)
    os_ref[tm:, half:] = edge(orp_ref, orn_ref)
    mixed = _dot(os_ref[...], wo_ref[...])
    x = x_ref[0] + gt1 * mixed[0:tm]
    o_ref[0] = x
    halo = jnp.concatenate([xp_ref[0], xn_ref[0]], axis=0) + gt1 * mixed[tm:]
    hs_ref[tm:tm + 2 * HALO, :] = (_rmsnorm(halo, gain) + shift).astype(BF16)
    u = []
    for r in range(n_sub):
        rows = slice(r * rp, (r + 1) * rp)
        hs_ref[rows, :] = (_rmsnorm(x[rows], gain) + shift).astype(BF16)
        u.append(up_proj(0, r))

    pending = [None] * n_sub
    acc_live = [False] * n_sub
    for j in range(n_inner):
        u_next = [None] * n_sub
        for r in range(n_sub):
            rows = slice(r * rp, (r + 1) * rp)
            if j + 1 < n_inner:
                u_next[r] = up_proj(j + 1, r)
            part = _dot(conv_act(u, j, r), wdn_ref[j])
            if pending[r] is None and j + 1 < n_inner:
                pending[r] = part
                continue
            total = part if pending[r] is None else pending[r] + part
            pending[r] = None
            if acc_live[r]:
                total = acc_ref[rows, :] + total
            if j + 1 < n_inner:
                acc_ref[rows, :] = total
                acc_live[r] = True
            else:
                o_ref[0, rows, :] = _rmsnorm(o_ref[0, rows, :] + gt_ref[0] * total, gf_ref[...])
        u = u_next


def _ffn(x, o_mla, o_ret, gt1, sh, sc, gt, g2, gf, wo, wup, cw, cb, wdn):
    bsz, seq, d = x.shape
    tm = min(FFN_TM, seq)
    mod_spec = pl.BlockSpec((1, 1, d), lambda b, i: (b, 0, 0))

    def tile(width):
        return pl.BlockSpec((1, tm, width), lambda b, i: (b, i, 0))

    def before(rows, width):
        return pl.BlockSpec((1, rows, width), lambda b, i: (b, jnp.maximum(i * (tm // rows) - 1, 0), 0))

    def after(rows, width):
        last = seq // rows - 1
        return pl.BlockSpec((1, rows, width), lambda b, i: (b, jnp.minimum((i + 1) * (tm // rows), last), 0))

    mw, rw = o_mla.shape[2], o_ret.shape[2]
    return pl.pallas_call(
        _ffn_kernel,
        out_shape=jax.ShapeDtypeStruct((bsz, seq, d), F32),
        grid=(bsz, seq // tm),
        in_specs=[tile(d), before(HALO, d), after(HALO, d),
                  tile(mw), before(BF16_ROWS, mw), after(BF16_ROWS, mw),
                  tile(rw), before(BF16_ROWS, rw), after(BF16_ROWS, rw),
                  mod_spec, mod_spec, mod_spec, mod_spec, _const_spec(g2.shape), _const_spec(gf.shape),
                  _const_spec(wo.shape),
                  _const_spec(wup.shape), _const_spec(cw.shape), _const_spec(cb.shape), _const_spec(wdn.shape)],
        out_specs=tile(d),
        scratch_shapes=[pltpu.VMEM((tm + 2 * HALO, d), BF16), pltpu.VMEM((tm + 2 * HALO, mw + rw), BF16),
                        pltpu.VMEM((tm, d), F32)],
        compiler_params=_params("parallel", "parallel"),
        name="ffn",
    )(x, x, x, o_mla, o_mla, o_mla, o_ret, o_ret, o_ret, gt1, sh, sc, gt, g2, gf, wo, wup, cw, cb, wdn)


def _rot_half_cols(w, half):
    shape = w.shape
    w = w.reshape(shape[:-1] + (shape[-1] // (2 * half), 2, half))
    return jnp.stack([-w[..., 1, :], w[..., 0, :]], axis=-2).reshape(shape)


def _rope_angles(pos, dim):
    inv = ROPE_BASE ** (-np.arange(0, dim, 2, dtype=np.float64) / dim)
    return pos.astype(np.float64)[:, None] * inv[None, :]


def _mla_tables(seq, with_pos):
    if not with_pos:
        return np.ones((seq, LANES), np.float32), np.zeros((seq, LANES), np.float32)
    pos = np.arange(seq)
    quarter = MLA_ROPE // 2
    ar = _rope_angles(pos // GRID_W, quarter)
    ac = _rope_angles(pos % GRID_W, quarter)
    ang = np.concatenate([ar, ar, ac, ac], axis=-1)
    pad = LANES - MLA_NOPE - MLA_ROPE
    cos = np.concatenate([np.ones((seq, MLA_NOPE)), np.cos(ang), np.ones((seq, pad))], axis=-1)
    sin = np.concatenate([np.zeros((seq, MLA_NOPE)), np.sin(ang), np.zeros((seq, pad))], axis=-1)
    return cos.astype(np.float32), sin.astype(np.float32)


def _ret_tables(seq, with_pos):
    if not with_pos:
        return np.ones((seq, RET_DK), np.float32), np.zeros((seq, RET_DK), np.float32)
    ang = _rope_angles(np.arange(seq), RET_DK)
    return (np.concatenate([np.cos(ang), np.cos(ang)], axis=-1).astype(np.float32),
            np.concatenate([-np.sin(ang), np.sin(ang)], axis=-1).astype(np.float32))


def _pad_heads(nope, rope):
    r = nope.shape[0]
    pad = jnp.zeros((r, MLA_HEADS, HEAD_PAD - MLA_NOPE - MLA_ROPE), nope.dtype)
    return jnp.concatenate([nope, rope, pad], axis=-1).reshape(r, MLA_HEADS * HEAD_PAD)


def _pair_up_kernel(a_ref, g_ref, o_ref):
    o_ref[0, :, :FFN_FC] = a_ref[0].astype(BF16)
    o_ref[0, :, FFN_FC:] = g_ref[0].astype(BF16)


def _pair_up_weights(w_up, layer, nj):
    d = w_up.shape[1]
    return pl.pallas_call(
        _pair_up_kernel,
        out_shape=jax.ShapeDtypeStruct((nj, d, 2 * FFN_FC), BF16),
        grid=(nj,),
        in_specs=[pl.BlockSpec((1, d, FFN_FC), lambda j: (layer, 0, j)),
                  pl.BlockSpec((1, d, FFN_FC), lambda j: (layer, 0, nj + j))],
        out_specs=pl.BlockSpec((1, d, 2 * FFN_FC), lambda j: (j, 0, 0)),
        compiler_params=_params("parallel"),
        name="pair_up_weights",
    )(w_up, w_up)


def _rope_placement():
    place = np.zeros((LANES, LANES), np.float32)
    rot = np.zeros((LANES, LANES), np.float32)
    half = MLA_ROPE // 4
    for c in range(MLA_ROPE):
        place[c, MLA_NOPE + c] = 1.0
        g, w = divmod(c, 2 * half)
        if w < half:
            rot[g * 2 * half + w + half, MLA_NOPE + c] = -1.0
        else:
            rot[g * 2 * half + w - half, MLA_NOPE + c] = 1.0
    return place, rot


def _win_kernel(w_ref, place_ref, rot_ref, o_ref):
    w = w_ref[0]
    o_ref[:, 0:COL_KPE] = w[:, 0:COL_KPE].astype(BF16)
    blk = w[:, COL_KPE:COL_KPE + LANES].astype(BF16)
    o_ref[:, COL_KPE:COL_KPE + LANES] = _dot(blk, place_ref[...]).astype(BF16)
    o_ref[:, COL_KPE_ROT:COL_KPE_ROT + LANES] = _dot(blk, rot_ref[...]).astype(BF16)
    o_ref[:, COL_RQ:] = w[:, COL_KPE + MLA_ROPE:].astype(BF16)


def _layout_w_in(w_in, layer):
    d, n = w_in.shape[1], w_in.shape[2]
    assert n - MLA_ROPE + 2 * LANES == COLS_IN
    rows = 256
    place, rot = _rope_placement()
    return pl.pallas_call(
        _win_kernel,
        out_shape=jax.ShapeDtypeStruct((d, COLS_IN), BF16),
        grid=(d // rows,),
        in_specs=[pl.BlockSpec((1, rows, n), lambda i: (layer, i, 0)),
                  _const_spec(place.shape), _const_spec(rot.shape)],
        out_specs=pl.BlockSpec((rows, COLS_IN), lambda i: (i, 0)),
        compiler_params=_params("parallel"),
        name="layout_w_in",
    )(w_in, place.astype(BF16), rot.astype(BF16))


def _layout_weights(w_in_stack, w_uq, w_ukv, w_up_stack, layer, conv_w, conv_b, w_down):
    q_rank = w_uq.shape[0]
    rot_half = MLA_ROPE // 4
    win = _layout_w_in(w_in_stack, layer)

    uq = w_uq.reshape(q_rank, MLA_HEADS, MLA_NOPE + MLA_ROPE)
    uq_n, uq_r = uq[..., :MLA_NOPE], uq[..., MLA_NOPE:]
    wuq = jnp.concatenate([_pad_heads(uq_n, uq_r),
                           _pad_heads(jnp.zeros_like(uq_n), _rot_half_cols(uq_r, rot_half))],
                          axis=-1).T.astype(BF16)

    ukv =w_ukv.reshape(w_ukv.shape[0], MLA_HEADS, MLA_NOPE + MLA_V)
    wk = _pad_heads(ukv[..., :MLA_NOPE], jnp.zeros(ukv.shape[:2] + (MLA_ROPE,), ukv.dtype)).astype(BF16)
    wv = ukv[..., MLA_NOPE:].reshape(w_ukv.shape[0], MLA_HEADS * MLA_V).T.astype(BF16)

    d_ff = w_down.shape[0]
    nj = d_ff // FFN_FC

    def pair_chunks(a):
        lead = a.shape[:-1]
        a = a.reshape(lead + (2, nj, FFN_FC))
        a = jnp.moveaxis(a, -2, 0)
        return a.reshape((nj,) + lead + (2 * FFN_FC,))

    wup = _pair_up_weights(w_up_stack, layer, nj)
    cw = pair_chunks(conv_w)
    cb = pair_chunks(conv_b[None, :])
    wdn = w_down.reshape(nj, FFN_FC, w_down.shape[1]).astype(BF16)
    return win, wuq, wk, wv, wup, cw, cb, wdn


def kernel(x, c, ctx, c_ctx, w_ada, b_ada, g_norm1, w_in, g_q, w_uq, g_kv, w_ukv, ret_decay, g_ret,
           w_out, g_norm2, w_up, conv_w, conv_b, w_down, g_final):
    bsz, seq, d = x.shape
    ctx_len = ctx.shape[1]
    depth = w_ada.shape[0]
    assert depth == 1, "single-layer block"
    l = 0

    rows = -(-(bsz + 1) // 8) * 8
    cs = jnp.concatenate([c, c_ctx[None, :], jnp.zeros((rows - bsz - 1, d), F32)], axis=0)
    mod = _adaln(cs, w_ada, b_ada[l][None, :], l)
    sh1, sc1, gt1, sh2, sc2, gt2 = [mod[:bsz, t * d:(t + 1) * d][:, None, :] for t in range(6)]
    shc1 = jnp.broadcast_to(mod[bsz, 0:d][None, None, :], (bsz, 1, d))
    scc1 = jnp.broadcast_to(mod[bsz, d:2 * d][None, None, :], (bsz, 1, d))

    win, wuq, wk, wv, wup, cw, cb, wdn = _layout_weights(w_in, w_uq[l], w_ukv[l], w_up, l, conv_w[l],
                                                       conv_b[l], w_down[l])
    g1 = g_norm1[l][None, :]
    gq = g_q[l][None, :]
    gkv = g_kv[l][None, :]

    cm, sm = _mla_tables(seq, True)
    cr, sr = _ret_tables(seq, True)
    q, k, v, rq, rk, rv, sg = _preproj(x, sh1, sc1, g1, win, gq, wuq, gkv, wk, wv, cm, sm, cr, sr,
                                       with_queries=True)
    cm0, sm0 = _mla_tables(ctx_len, False)
    cr0, sr0 = _ret_tables(ctx_len, False)
    k_c, v_c, rk_c, rv_c = _preproj(ctx, shc1, scc1, g1, win, gq, wuq, gkv, wk, wv, cm0, sm0, cr0, sr0,
                                    with_queries=False)

    o_mla = _mla_attn(q, k, k_c, v, v_c)
    rd = jnp.broadcast_to(jnp.transpose(ret_decay[l])[:, :, None], (RET_HEADS, 2, RET_CHUNK)).astype(F32)
    o_ret = _retention(rd, rq, rk, rv, sg, rk_c, rv_c, g_ret[l][None, :])

    return _ffn(x, o_mla, o_ret, gt1, sh2, sc2, gt2, g_norm2[l][None, :], g_final[None, :],
                w_out[l].astype(BF16), wup, cw, cb, wdn)
```

```python
import functools

import jax
import jax.numpy as jnp
import numpy as np
from jax import lax
from jax.experimental import pallas as pl
from jax.experimental.pallas import tpu as pltpu

F32 = jnp.float32
BF16 = jnp.bfloat16

GRID_W = 64
MLA_HEADS = 8
MLA_NOPE = 64
MLA_ROPE = 32
MLA_V = 64
MLA_SCALE = (MLA_NOPE + MLA_ROPE) ** -0.5
Q_SCALE = MLA_SCALE * float(np.log2(np.e))
SCORE_FLOOR = -1e30
RET_HEADS = 4
RET_DK = 128
RET_DV = 128
ROPE_BASE = 10000.0
EPS = 1e-6

LANES = 128
MXU_DIM = 256
BF16_ROWS = 16
VT_ROWS = MLA_V + BF16_ROWS
VMEM_LIMIT_BYTES = 56 * 1024 * 1024

HEAD_PAD = LANES
PRE_TM = 512
PRE_SUBTILES = 2
ATT_TQ = 512
ATT_SUBTILES = 2
ATT_KB = MXU_DIM
ATT_PV_LAG = 1
RET_CHUNK = 256
FFN_TM = 512
FFN_FC = 256
FFN_SUBTILES = 4
HALO = 8

NT_DIMS = (((1,), (1,)), ((), ()))
TN_DIMS = (((0,), (0,)), ((), ()))


def _dot(a, b):
    return jnp.dot(a, b, preferred_element_type=F32)


def _rmsnorm(x, g):
    return x * lax.rsqrt(jnp.mean(x * x, axis=-1, keepdims=True) + EPS) * g


def _silu(x):
    return x * (1.0 / (1.0 + jnp.exp(-x)))


def _params(*semantics):
    return pltpu.CompilerParams(dimension_semantics=semantics, vmem_limit_bytes=VMEM_LIMIT_BYTES)


def _const_spec(shape):
    zeros = (0,) * len(shape)
    return pl.BlockSpec(shape, lambda *_: zeros, pipeline_mode=pl.Buffered(1))


def _adaln_kernel(c_ref, w_ref, b_ref, o_ref):
    a = _silu(c_ref[...])
    o_ref[...] = _dot(a.astype(BF16), w_ref[0].astype(BF16)) + b_ref[...]


def _adaln(cs, w, b, layer):
    rows, d = cs.shape
    n = w.shape[2]
    tn = 1024
    return pl.pallas_call(
        _adaln_kernel,
        out_shape=jax.ShapeDtypeStruct((rows, n), F32),
        grid=(n // tn,),
        in_specs=[pl.BlockSpec((rows, d), lambda j: (0, 0)),
                  pl.BlockSpec((1, d, tn), lambda j: (layer, 0, j)),
                  pl.BlockSpec((1, tn), lambda j: (0, j))],
        out_specs=pl.BlockSpec((rows, tn), lambda j: (0, j)),
        compiler_params=_params("parallel"),
        name="adaln",
    )(cs, w, b)


COL_CQ = 0
COL_CKV = 256
COL_KPE = 512
COL_KPE_ROT = 640
COL_RQ = 768
COL_RK = 1280
COL_RV = 1792
COL_RG = 2304
COLS_IN = 2816
RET_W = RET_HEADS * RET_DK


def _preproj_kernel(with_queries, x_ref, sh_ref, sc_ref, g1_ref, win_ref, gq_ref, wuq_ref,
                    gkv_ref, wk_ref, wv_ref, cm_ref, sm_ref, cmt_ref, smt_ref, cr_ref, sr_ref, *out_refs):
    if with_queries:
        q_ref, k_ref, v_ref, rq_ref, rk_ref, rv_ref, sg_ref = out_refs
    else:
        k_ref, v_ref, rk_ref, rv_ref = out_refs
    n_sub = max(1, min(PRE_SUBTILES, x_ref.shape[1] // MXU_DIM))
    rp = x_ref.shape[1] // n_sub
    gain = g1_ref[...] * (1.0 + sc_ref[0])
    shift = sh_ref[0]

    projs = []
    for r in range(n_sub):
        rows = slice(r * rp, (r + 1) * rp)
        h = _rmsnorm(x_ref[0, rows, :], gain) + shift
        projs.append(_dot(h.astype(BF16), win_ref[...]))

    for r in range(n_sub):
        rows = slice(r * rp, (r + 1) * rp)
        proj = projs[r]
        cm = cm_ref[rows, :]
        sm = sm_ref[rows, :]
        cr = cr_ref[rows, :]
        sr = sr_ref[rows, :]

        ckvn = _rmsnorm(proj[:, COL_CKV:COL_CKV + 256], gkv_ref[...]).astype(BF16)
        kn = _dot(ckvn, wk_ref[...])
        vt = lax.dot_general(wv_ref[...], ckvn, NT_DIMS, preferred_element_type=F32).astype(BF16)
        ones_rows = jnp.ones((BF16_ROWS, rp), BF16)
        for hd in range(MLA_HEADS):
            v_ref[0, hd * VT_ROWS:hd * VT_ROWS + MLA_V, rows] = vt[hd * MLA_V:(hd + 1) * MLA_V]
            v_ref[0, hd * VT_ROWS + MLA_V:(hd + 1) * VT_ROWS, rows] = ones_rows
        kpe = proj[:, COL_KPE:COL_KPE + LANES] * cm + proj[:, COL_KPE_ROT:COL_KPE_ROT + LANES] * sm
        for hd in range(MLA_HEADS):
            sl = slice(hd * HEAD_PAD, (hd + 1) * HEAD_PAD)
            k_ref[0, rows, sl] = (kn[:, sl] + kpe).astype(BF16)

        for hd in range(RET_HEADS):
            sl = slice(hd * RET_DK, (hd + 1) * RET_DK)
            b = proj[:, COL_RK + hd * RET_DK:COL_RK + (hd + 1) * RET_DK]
            rk_ref[0, rows, sl] = ((b * cr + pltpu.roll(b, RET_DK // 2, 1) * sr) * (RET_DK ** -0.5)).astype(BF16)
        rv_ref[0, rows, :] = proj[:, COL_RV:COL_RV + RET_W].astype(BF16)

        if with_queries:
            cqn = _rmsnorm(proj[:, COL_CQ:COL_CQ + 256], gq_ref[...]).astype(BF16)
            qa = lax.dot_general(wuq_ref[...], cqn, NT_DIMS, preferred_element_type=F32)
            cmt = cmt_ref[:, rows]
            smt = smt_ref[:, rows]
            rot0 = MLA_HEADS * HEAD_PAD
            for hd in range(MLA_HEADS):
                sl = slice(hd * HEAD_PAD, (hd + 1) * HEAD_PAD)
                a = qa[sl, :]
                b = qa[rot0 + hd * HEAD_PAD:rot0 + (hd + 1) * HEAD_PAD, :]
                q_ref[0, sl, rows] = ((a * cmt + b * smt) * Q_SCALE).astype(BF16)
            for hd in range(RET_HEADS):
                sl = slice(hd * RET_DK, (hd + 1) * RET_DK)
                a = proj[:, COL_RQ + hd * RET_DK:COL_RQ + (hd + 1) * RET_DK]
                rq_ref[0, rows, sl] = (a * cr + pltpu.roll(a, RET_DK // 2, 1) * sr).astype(BF16)
            sg_ref[0, rows, :] = _silu(proj[:, COL_RG:COL_RG + RET_W]).astype(BF16)


def _preproj(x, sh, sc, g1, win, gq, wuq, gkv, wk, wv, cm, sm, cr, sr, *, with_queries):
    bsz, seq, d = x.shape
    tm = min(PRE_TM, seq)
    qk_w = MLA_HEADS * HEAD_PAD
    v_w = MLA_HEADS * VT_ROWS
    cmt, smt = cm.T, sm.T
    tabt_spec = pl.BlockSpec((LANES, tm), lambda b, i: (0, i))
    qt_shape = jax.ShapeDtypeStruct((bsz, qk_w, seq), BF16)
    qt_spec = pl.BlockSpec((1, qk_w, tm), lambda b, i: (b, 0, i))

    def tok(width):
        return pl.BlockSpec((1, tm, width), lambda b, i: (b, i, 0))

    def out(width):
        return jax.ShapeDtypeStruct((bsz, seq, width), BF16)

    mod_spec = pl.BlockSpec((1, 1, d), lambda b, i: (b, 0, 0))
    tab_spec = pl.BlockSpec((tm, LANES), lambda b, i: (i, 0))
    vt_shape = jax.ShapeDtypeStruct((bsz, v_w, seq), BF16)
    vt_spec = pl.BlockSpec((1, v_w, tm), lambda b, i: (b, 0, i))
    if with_queries:
        out_shape = [qt_shape, out(qk_w), vt_shape, out(RET_W), out(RET_W), out(RET_W), out(RET_W)]
        out_specs = [qt_spec, tok(qk_w), vt_spec, tok(RET_W), tok(RET_W), tok(RET_W), tok(RET_W)]
    else:
        out_shape = [out(qk_w), vt_shape, out(RET_W), out(RET_W)]
        out_specs = [tok(qk_w), vt_spec, tok(RET_W), tok(RET_W)]
    return pl.pallas_call(
        functools.partial(_preproj_kernel, with_queries),
        out_shape=out_shape,
        grid=(bsz, seq // tm),
        in_specs=[tok(d), mod_spec, mod_spec, _const_spec(g1.shape), _const_spec(win.shape),
                  _const_spec(gq.shape), _const_spec(wuq.shape), _const_spec(gkv.shape),
                  _const_spec(wk.shape), _const_spec(wv.shape), tab_spec, tab_spec, tabt_spec, tabt_spec,
                  tab_spec, tab_spec],
        out_specs=out_specs,
        compiler_params=_params("parallel", "parallel"),
        name="preproj_q" if with_queries else "preproj_ctx",
    )(x, sh, sc, g1, win, gq, wuq, gkv, wk, wv, cm, sm, cmt, smt, cr, sr)


def _mla_attn_kernel(q_ref, kl_ref, kc_ref, vl_ref, vc_ref, o_ref, s_ref):
    lat_len, ctx_len = kl_ref.shape[1], kc_ref.shape[1]
    kb_ctx = min(ATT_KB, ctx_len)
    blocks = ([(False, k0, ATT_KB) for k0 in range(0, lat_len, ATT_KB)]
              + [(True, k0, kb_ctx) for k0 in range(0, ctx_len, kb_ctx)])
    n_blk = len(blocks)
    pair_rows = 2 * MLA_V

    def s_rows(b):
        from_ctx, k0, size = blocks[b]
        k0 += lat_len if from_ctx else 0
        return slice(k0, k0 + size)

    def k_block(h, b):
        from_ctx, k0, size = blocks[b]
        ref = kc_ref if from_ctx else kl_ref
        return ref[0, k0:k0 + size, h * HEAD_PAD:(h + 1) * HEAD_PAD]

    def vt_block(h, b):
        from_ctx, k0, size = blocks[b]
        ref = vc_ref if from_ctx else vl_ref
        return ref[0, h * VT_ROWS:(h + 1) * VT_ROWS, k0:k0 + size]

    tq = s_ref.shape[2]
    n_rows = (q_ref.shape[2] // tq) * MLA_HEADS
    n_units = n_rows * n_blk
    exp_lag = n_blk
    pv_lag = exp_lag + ATT_PV_LAG
    m_run = [None] * n_rows
    ot = [None] * n_rows
    p_blocks = {}
    bm = None
    for t in range(n_units + pv_lag):
        bm_tie, bm = bm, None
        if t < n_units:
            r, b = divmod(t, n_blk)
            qi, h = divmod(r, MLA_HEADS)
            qt = q_ref[0, h * HEAD_PAD:(h + 1) * HEAD_PAD, qi * tq:(qi + 1) * tq]
            s = _dot(k_block(h, b), qt)
            s_ref[r % 2, s_rows(b), :] = s
            bm = jnp.max(s, axis=0, keepdims=True)
            m_run[r] = bm if m_run[r] is None else jnp.maximum(m_run[r], bm)
        e = t - exp_lag
        if 0 <= e < n_units:
            r, b = divmod(e, n_blk)
            m_use = m_run[r]
            if bm_tie is not None:
                m_use = jnp.maximum(m_use, jnp.minimum(bm_tie, SCORE_FLOOR))
            p_blocks[e] = jnp.exp2(s_ref[r % 2, s_rows(b), :] - m_use).astype(BF16)
        v = t - pv_lag
        if 0 <= v < n_units:
            r, b = divmod(v, n_blk)
            qi, h = divmod(r, MLA_HEADS)
            pv = _dot(vt_block(h, b), p_blocks.pop(v))
            ot[r] = pv if ot[r] is None else ot[r] + pv
            if b == n_blk - 1:
                o_h = ot[r][:MLA_V] * (1.0 / ot[r][MLA_V:MLA_V + 1])
                ot[r] = o_h
                if h % 2 == 1:
                    pair_t = jnp.concatenate([ot[r - 1], o_h], axis=0)
                    cols = slice((h // 2) * pair_rows, (h // 2 + 1) * pair_rows)
                    o_ref[0, qi * tq:(qi + 1) * tq, cols] = pair_t.T.astype(BF16)
                    ot[r - 1] = ot[r] = None


def _mla_attn(qt, k_lat, k_ctx, vt_lat, vt_ctx):
    bsz, qk_w, seq = qt.shape
    ctx_len = k_ctx.shape[1]
    v_w = vt_lat.shape[1]
    o_w = MLA_HEADS * MLA_V
    tq = min(ATT_TQ, seq)
    tstep = min(ATT_TQ * ATT_SUBTILES, seq)
    assert seq % ATT_KB == 0 and ctx_len % min(ATT_KB, ctx_len) == 0 and tstep % tq == 0
    return pl.pallas_call(
        _mla_attn_kernel,
        out_shape=jax.ShapeDtypeStruct((bsz, seq, o_w), BF16),
        grid=(bsz, seq // tstep),
        in_specs=[pl.BlockSpec((1, qk_w, tstep), lambda b, i: (b, 0, i)),
                  pl.BlockSpec((1, seq, qk_w), lambda b, i: (b, 0, 0)),
                  pl.BlockSpec((1, ctx_len, qk_w), lambda b, i: (b, 0, 0)),
                  pl.BlockSpec((1, v_w, seq), lambda b, i: (b, 0, 0)),
                  pl.BlockSpec((1, v_w, ctx_len), lambda b, i: (b, 0, 0))],
        out_specs=pl.BlockSpec((1, tstep, o_w), lambda b, i: (b, i, 0)),
        scratch_shapes=[pltpu.VMEM((2, seq + ctx_len, tq), F32)],
        compiler_params=_params("parallel", "parallel"),
        name="mla_attn",
    )(qt, k_lat, k_ctx, vt_lat, vt_ctx)


def _retention_kernel(rd_ref, q_ref, k_ref, v_ref, sg_ref, kc_ref, vc_ref, g_ref, o_ref, pf_ref, sb_ref):
    c = RET_CHUNK
    n_chunks = q_ref.shape[1] // c
    row = lax.broadcasted_iota(jnp.int32, (c, RET_DV), 0).astype(F32)
    diff = (lax.broadcasted_iota(jnp.int32, (c, c), 0) - lax.broadcasted_iota(jnp.int32, (c, c), 1)).astype(F32)

    for hd in range(RET_HEADS):
        hs = slice(hd * RET_DK, (hd + 1) * RET_DK)
        lgf = -jnp.exp(rd_ref[hd, 0:1, :])
        lgb = -jnp.exp(rd_ref[hd, 1:2, :])
        lgf1 = lgf[:, :RET_DV]
        lgb1 = lgb[:, :RET_DV]
        zeta_f = jnp.exp(lgf1 * (c - 1.0 - row))
        zeta_b = jnp.exp(lgb1 * row)
        xi_f = jnp.exp(lgf1 * (row + 1.0))
        xi_b = jnp.exp(lgb1 * (c - row))
        gc_f = jnp.exp(lgf1 * float(c))
        gc_b = jnp.exp(lgb1 * float(c))
        dmat = jnp.where(diff >= 0.0, jnp.exp(lgf * jnp.maximum(diff, 0.0)), jnp.exp(lgb * jnp.maximum(-diff, 0.0)))

        def kv_outer(kk, vv, zeta_f=zeta_f, zeta_b=zeta_b):
            vf = vv.astype(F32)
            v2 = jnp.concatenate([vf * zeta_f, vf * zeta_b], axis=1).astype(BF16)
            return lax.dot_general(kk, v2, TN_DIMS, preferred_element_type=F32)

        p_ctx = kv_outer(kc_ref[0, :, hs], vc_ref[0, :, hs])
        sf = p_ctx[:, :RET_DV]
        sb = p_ctx[:, RET_DV:]

        for ci in reversed(range(n_chunks)):
            rows = slice(ci * c, (ci + 1) * c)
            p = kv_outer(k_ref[0, rows, hs], v_ref[0, rows, hs])
            pf_ref[hd * n_chunks + ci] = p[:, :RET_DV]
            sb_ref[hd * n_chunks + ci] = sb
            sb = gc_b * sb + p[:, RET_DV:]

        g = g_ref[:, hs]
        for ci in range(n_chunks):
            rows = slice(ci * c, (ci + 1) * c)
            qc = q_ref[0, rows, hs]
            s = lax.dot_general(qc, k_ref[0, rows, hs], NT_DIMS, preferred_element_type=F32)
            inner = _dot((s * dmat).astype(BF16), v_ref[0, rows, hs])
            st = jnp.concatenate([sf, sb_ref[hd * n_chunks + ci]], axis=1).astype(BF16)
            cross = _dot(qc, st)
            y = inner + cross[:, :RET_DV] * xi_f + cross[:, RET_DV:] * xi_b
            mu = jnp.mean(y, axis=-1, keepdims=True)
            d = y - mu
            var = jnp.mean(d * d, axis=-1, keepdims=True)
            yn = d * lax.rsqrt(var + EPS) * g
            o_ref[0, rows, hs] = (sg_ref[0, rows, hs].astype(F32) * yn).astype(BF16)
            sf = gc_f * sf + pf_ref[hd * n_chunks + ci]


def _retention(rd, rq, rk, rv, sg, rk_ctx, rv_ctx, g_ret):
    bsz, seq, width = rq.shape
    ctx_len = rk_ctx.shape[1]
    assert ctx_len == RET_CHUNK and seq % RET_CHUNK == 0

    def tokens(rows):
        return pl.BlockSpec((1, rows, width), lambda b: (b, 0, 0))

    n_states = RET_HEADS * (seq // RET_CHUNK)
    return pl.pallas_call(
        _retention_kernel,
        out_shape=jax.ShapeDtypeStruct((bsz, seq, width), BF16),
        grid=(bsz,),
        in_specs=[_const_spec(rd.shape), tokens(seq), tokens(seq), tokens(seq), tokens(seq),
                  tokens(ctx_len), tokens(ctx_len), _const_spec(g_ret.shape)],
        out_specs=tokens(seq),
        scratch_shapes=[pltpu.VMEM((n_states, RET_DK, RET_DV), F32), pltpu.VMEM((n_states, RET_DK, RET_DV), F32)],
        compiler_params=_params("parallel"),
        name="retention",
    )(rd, rq, rk, rv, sg, rk_ctx, rv_ctx, g_ret)


def _ffn_kernel(x_ref, xp_ref, xn_ref, om_ref, omp_ref, omn_ref, or_ref, orp_ref, orn_ref,
                gt1_ref, sh_ref, sc_ref, gt_ref, g2_ref, gf_ref, wo_ref,
                wup_ref, cw_ref, cb_ref, wdn_ref, o_ref, hs_ref, os_ref, acc_ref):
    tm = x_ref.shape[1]
    n_inner = wup_ref.shape[0]
    half = om_ref.shape[2]
    i = pl.program_id(1)

    def edge(before_ref, after_ref):
        return jnp.concatenate([before_ref[0].astype(F32)[BF16_ROWS - HALO:], after_ref[0].astype(F32)[:HALO]],
                               axis=0).astype(BF16)

    gt1 = gt1_ref[0]
    gain = g2_ref[...] * (1.0 + sc_ref[0])
    shift = sh_ref[0]
    has_prev = (i > 0).astype(F32)
    has_next = (i < pl.num_programs(1) - 1).astype(F32)
    row8 = lax.broadcasted_iota(jnp.int32, (HALO, 2 * FFN_FC), 0)

    n_sub = FFN_SUBTILES
    rp = tm // n_sub
    last = n_sub - 1

    def up_proj(j, r):
        rows = slice(r * rp, (r + 1) * rp + (2 * HALO if r == last else 0))
        return _dot(hs_ref[rows, :], wup_ref[j])

    def conv_act(u, j, r):
        um = u[r][0:rp]
        if r == 0:
            u_before = u[last][rp + HALO - 1:rp + HALO] * has_prev
        else:
            u_before = u[r - 1][rp - 1:rp]
        if r == last:
            u_after = u[last][rp + HALO:rp + HALO + 1] * has_next
        else:
            u_after = u[r + 1][0:1]
        up = pltpu.roll(um, 1, 0)
        up = jnp.concatenate([jnp.where(row8 == 0, u_before, up[0:HALO]), up[HALO:]], axis=0)
        un = pltpu.roll(um, rp - 1, 0)
        un = jnp.concatenate([un[:rp - HALO], jnp.where(row8 == HALO - 1, u_after, un[rp - HALO:])], axis=0)
        cw = cw_ref[j]
        cv = up * cw[0:1] + um * cw[1:2] + un * cw[2:3] + cb_ref[j]
        return (_silu(cv[:, FFN_FC:]) * cv[:, :FFN_FC]).astype(BF16)

    os_ref[0:tm, 0:half] = om_ref[0]
    os_ref[0:tm, half:] = or_ref[0]
    os_ref[tm:, 0:half] = edge(omp_ref, omn_ref)
    os_ref[tm:, half:] = edge(orp_ref, orn_ref)
    mixed = _dot(os_ref[...], wo_ref[...])
    x = x_ref[0] + gt1 * mixed[0:tm]
    o_ref[0] = x
    halo = jnp.concatenate([xp_ref[0], xn_ref[0]], axis=0) + gt1 * mixed[tm:]
    hs_ref[tm:tm + 2 * HALO, :] = (_rmsnorm(halo, gain) + shift).astype(BF16)
    u = []
    for r in range(n_sub):
        rows = slice(r * rp, (r + 1) * rp)
        hs_ref[rows, :] = (_rmsnorm(x[rows], gain) + shift).astype(BF16)
        u.append(up_proj(0, r))

    pending = [None] * n_sub
    acc_live = [False] * n_sub
    for j in range(n_inner):
        u_next = [None] * n_sub
        for r in range(n_sub):
            rows = slice(r * rp, (r + 1) * rp)
            if j + 1 < n_inner:
                u_next[r] = up_proj(j + 1, r)
            part = _dot(conv_act(u, j, r), wdn_ref[j])
            if pending[r] is None and j + 1 < n_inner:
                pending[r] = part
                continue
            total = part if pending[r] is None else pending[r] + part
            pending[r] = None
            if acc_live[r]:
                total = acc_ref[rows, :] + total
            if j + 1 < n_inner:
                acc_ref[rows, :] = total
                acc_live[r] = True
            else:
                o_ref[0, rows, :] = _rmsnorm(o_ref[0, rows, :] + gt_ref[0] * total, gf_ref[...])
        u = u_next


def _ffn(x, o_mla, o_ret, gt1, sh, sc, gt, g2, gf, wo, wup, cw, cb, wdn):
    bsz, seq, d = x.shape
    tm = min(FFN_TM, seq)
    mod_spec = pl.BlockSpec((1, 1, d), lambda b, i: (b, 0, 0))

    def tile(width):
        return pl.BlockSpec((1, tm, width), lambda b, i: (b, i, 0))

    def before(rows, width):
        return pl.BlockSpec((1, rows, width), lambda b, i: (b, jnp.maximum(i * (tm // rows) - 1, 0), 0))

    def after(rows, width):
        last = seq // rows - 1
        return pl.BlockSpec((1, rows, width), lambda b, i: (b, jnp.minimum((i + 1) * (tm // rows), last), 0))

    mw, rw = o_mla.shape[2], o_ret.shape[2]
    return pl.pallas_call(
        _ffn_kernel,
        out_shape=jax.ShapeDtypeStruct((bsz, seq, d), F32),
        grid=(bsz, seq // tm),
        in_specs=[tile(d), before(HALO, d), after(HALO, d),
                  tile(mw), before(BF16_ROWS, mw), after(BF16_ROWS, mw),
                  tile(rw), before(BF16_ROWS, rw), after(BF16_ROWS, rw),
                  mod_spec, mod_spec, mod_spec, mod_spec, _const_spec(g2.shape), _const_spec(gf.shape),
                  _const_spec(wo.shape),
                  _const_spec(wup.shape), _const_spec(cw.shape), _const_spec(cb.shape), _const_spec(wdn.shape)],
        out_specs=tile(d),
        scratch_shapes=[pltpu.VMEM((tm + 2 * HALO, d), BF16), pltpu.VMEM((tm + 2 * HALO, mw + rw), BF16),
                        pltpu.VMEM((tm, d), F32)],
        compiler_params=_params("parallel", "parallel"),
        name="ffn",
    )(x, x, x, o_mla, o_mla, o_mla, o_ret, o_ret, o_ret, gt1, sh, sc, gt, g2, gf, wo, wup, cw, cb, wdn)


def _rot_half_cols(w, half):
    shape = w.shape
    w = w.reshape(shape[:-1] + (shape[-1] // (2 * half), 2, half))
    return jnp.stack([-w[..., 1, :], w[..., 0, :]], axis=-2).reshape(shape)


def _rope_angles(pos, dim):
    inv = ROPE_BASE ** (-np.arange(0, dim, 2, dtype=np.float64) / dim)
    return pos.astype(np.float64)[:, None] * inv[None, :]


def _mla_tables(seq, with_pos):
    if not with_pos:
        return np.ones((seq, LANES), np.float32), np.zeros((seq, LANES), np.float32)
    pos = np.arange(seq)
    quarter = MLA_ROPE // 2
    ar = _rope_angles(pos // GRID_W, quarter)
    ac = _rope_angles(pos % GRID_W, quarter)
    ang = np.concatenate([ar, ar, ac, ac], axis=-1)
    pad = LANES - MLA_NOPE - MLA_ROPE
    cos = np.concatenate([np.ones((seq, MLA_NOPE)), np.cos(ang), np.ones((seq, pad))], axis=-1)
    sin = np.concatenate([np.zeros((seq, MLA_NOPE)), np.sin(ang), np.zeros((seq, pad))], axis=-1)
    return cos.astype(np.float32), sin.astype(np.float32)


def _ret_tables(seq, with_pos):
    if not with_pos:
        return np.ones((seq, RET_DK), np.float32), np.zeros((seq, RET_DK), np.float32)
    ang = _rope_angles(np.arange(seq), RET_DK)
    return (np.concatenate([np.cos(ang), np.cos(ang)], axis=-1).astype(np.float32),
            np.concatenate([-np.sin(ang), np.sin(ang)], axis=-1).astype(np.float32))


def _pad_heads(nope, rope):
    r = nope.shape[0]
    pad = jnp.zeros((r, MLA_HEADS, HEAD_PAD - MLA_NOPE - MLA_ROPE), nope.dtype)
    return jnp.concatenate([nope, rope, pad], axis=-1).reshape(r, MLA_HEADS * HEAD_PAD)


def _pair_up_kernel(a_ref, g_ref, o_ref):
    o_ref[0, :, :FFN_FC] = a_ref[0].astype(BF16)
    o_ref[0, :, FFN_FC:] = g_ref[0].astype(BF16)


def _pair_up_weights(w_up, layer, nj):
    d = w_up.shape[1]
    return pl.pallas_call(
        _pair_up_kernel,
        out_shape=jax.ShapeDtypeStruct((nj, d, 2 * FFN_FC), BF16),
        grid=(nj,),
        in_specs=[pl.BlockSpec((1, d, FFN_FC), lambda j: (layer, 0, j)),
                  pl.BlockSpec((1, d, FFN_FC), lambda j: (layer, 0, nj + j))],
        out_specs=pl.BlockSpec((1, d, 2 * FFN_FC), lambda j: (j, 0, 0)),
        compiler_params=_params("parallel"),
        name="pair_up_weights",
    )(w_up, w_up)


def _rope_placement():
    place = np.zeros((LANES, LANES), np.float32)
    rot = np.zeros((LANES, LANES), np.float32)
    half = MLA_ROPE // 4
    for c in range(MLA_ROPE):
        place[c, MLA_NOPE + c] = 1.0
        g, w = divmod(c, 2 * half)
        if w < half:
            rot[g * 2 * half + w + half, MLA_NOPE + c] = -1.0
        else:
            rot[g * 2 * half + w - half, MLA_NOPE + c] = 1.0
    return place, rot


def _win_kernel(w_ref, place_ref, rot_ref, o_ref):
    w = w_ref[0]
    o_ref[:, 0:COL_KPE] = w[:, 0:COL_KPE].astype(BF16)
    blk = w[:, COL_KPE:COL_KPE + LANES].astype(BF16)
    o_ref[:, COL_KPE:COL_KPE + LANES] = _dot(blk, place_ref[...]).astype(BF16)
    o_ref[:, COL_KPE_ROT:COL_KPE_ROT + LANES] = _dot(blk, rot_ref[...]).astype(BF16)
    o_ref[:, COL_RQ:] = w[:, COL_KPE + MLA_ROPE:].astype(BF16)


def _layout_w_in(w_in, layer):
    d, n = w_in.shape[1], w_in.shape[2]
    assert n - MLA_ROPE + 2 * LANES == COLS_IN
    rows = 256
    place, rot = _rope_placement()
    return pl.pallas_call(
        _win_kernel,
        out_shape=jax.ShapeDtypeStruct((d, COLS_IN), BF16),
        grid=(d // rows,),
        in_specs=[pl.BlockSpec((1, rows, n), lambda i: (layer, i, 0)),
                  _const_spec(place.shape), _const_spec(rot.shape)],
        out_specs=pl.BlockSpec((rows, COLS_IN), lambda i: (i, 0)),
        compiler_params=_params("parallel"),
        name="layout_w_in",
    )(w_in, place.astype(BF16), rot.astype(BF16))


def _layout_weights(w_in_stack, w_uq, w_ukv, w_up_stack, layer, conv_w, conv_b, w_down):
    q_rank = w_uq.shape[0]
    rot_half = MLA_ROPE // 4
    win = _layout_w_in(w_in_stack, layer)

    uq = w_uq.reshape(q_rank, MLA_HEADS, MLA_NOPE + MLA_ROPE)
    uq_n, uq_r = uq[..., :MLA_NOPE], uq[..., MLA_NOPE:]
    wuq = jnp.concatenate([_pad_heads(uq_n, uq_r),
                           _pad_heads(jnp.zeros_like(uq_n), _rot_half_cols(uq_r, rot_half))],
                          axis=-1).T.astype(BF16)

    ukv =w_ukv.reshape(w_ukv.shape[0], MLA_HEADS, MLA_NOPE + MLA_V)
    wk = _pad_heads(ukv[..., :MLA_NOPE], jnp.zeros(ukv.shape[:2] + (MLA_ROPE,), ukv.dtype)).astype(BF16)
    wv = ukv[..., MLA_NOPE:].reshape(w_ukv.shape[0], MLA_HEADS * MLA_V).T.astype(BF16)

    d_ff = w_down.shape[0]
    nj = d_ff // FFN_FC

    def pair_chunks(a):
        lead = a.shape[:-1]
        a = a.reshape(lead + (2, nj, FFN_FC))
        a = jnp.moveaxis(a, -2, 0)
        return a.reshape((nj,) + lead + (2 * FFN_FC,))

    wup = _pair_up_weights(w_up_stack, layer, nj)
    cw = pair_chunks(conv_w)
    cb = pair_chunks(conv_b[None, :])
    wdn = w_down.reshape(nj, FFN_FC, w_down.shape[1]).astype(BF16)
    return win, wuq, wk, wv, wup, cw, cb, wdn


def kernel(x, c, ctx, c_ctx, w_ada, b_ada, g_norm1, w_in, g_q, w_uq, g_kv, w_ukv, ret_decay, g_ret,
           w_out, g_norm2, w_up, conv_w, conv_b, w_down, g_final):
    bsz, seq, d = x.shape
    ctx_len = ctx.shape[1]
    depth = w_ada.shape[0]
    assert depth == 1, "single-layer block"
    l = 0

    rows = -(-(bsz + 1) // 8) * 8
    cs = jnp.concatenate([c, c_ctx[None, :], jnp.zeros((rows - bsz - 1, d), F32)], axis=0)
    mod = _adaln(cs, w_ada, b_ada[l][None, :], l)
    sh1, sc1, gt1, sh2, sc2, gt2 = [mod[:bsz, t * d:(t + 1) * d][:, None, :] for t in range(6)]
    shc1 = jnp.broadcast_to(mod[bsz, 0:d][None, None, :], (bsz, 1, d))
    scc1 = jnp.broadcast_to(mod[bsz, d:2 * d][None, None, :], (bsz, 1, d))

    win, wuq, wk, wv, wup, cw, cb, wdn = _layout_weights(w_in, w_uq[l], w_ukv[l], w_up, l, conv_w[l],
                                                       conv_b[l], w_down[l])
    g1 = g_norm1[l][None, :]
    gq = g_q[l][None, :]
    gkv = g_kv[l][None, :]

    cm, sm = _mla_tables(seq, True)
    cr, sr = _ret_tables(seq, True)
    q, k, v, rq, rk, rv, sg = _preproj(x, sh1, sc1, g1, win, gq, wuq, gkv, wk, wv, cm, sm, cr, sr,
                                       with_queries=True)
    cm0, sm0 = _mla_tables(ctx_len, False)
    cr0, sr0 = _ret_tables(ctx_len, False)
    k_c, v_c, rk_c, rv_c = _preproj(ctx, shc1, scc1, g1, win, gq, wuq, gkv, wk, wv, cm0, sm0, cr0, sr0,
                                    with_queries=False)

    o_mla = _mla_attn(q, k, k_c, v, v_c)
    rd = jnp.broadcast_to(jnp.transpose(ret_decay[l])[:, :, None], (RET_HEADS, 2, RET_CHUNK)).astype(F32)
    o_ret = _retention(rd, rq, rk, rv, sg, rk_c, rv_c, g_ret[l][None, :])

    return _ffn(x, o_mla, o_ret, gt1, sh2, sc2, gt2, g_norm2[l][None, :], g_final[None, :],
                w_out[l].astype(BF16), wup, cw, cb, wdn)
```

```python
import functools

import jax
import jax.numpy as jnp
import numpy as np
from jax import lax
from jax.experimental import pallas as pl
from jax.experimental.pallas import tpu as pltpu

F32 = jnp.float32
BF16 = jnp.bfloat16

GRID_W = 64
MLA_HEADS = 8
MLA_NOPE = 64
MLA_ROPE = 32
MLA_V = 64
MLA_SCALE = (MLA_NOPE + MLA_ROPE) ** -0.5
Q_SCALE = MLA_SCALE * float(np.log2(np.e))
SCORE_FLOOR = -1e30
RET_HEADS = 4
RET_DK = 128
RET_DV = 128
ROPE_BASE = 10000.0
EPS = 1e-6

LANES = 128
MXU_DIM = 256
BF16_ROWS = 16
VT_ROWS = MLA_V + BF16_ROWS
VMEM_LIMIT_BYTES = 56 * 1024 * 1024

HEAD_PAD = LANES
PRE_TM = 512
PRE_SUBTILES = 2
ATT_TQ = 512
ATT_SUBTILES = 2
ATT_KB = MXU_DIM
ATT_PV_LAG = 2
RET_CHUNK = 256
FFN_TM = 512
FFN_FC = 256
FFN_SUBTILES = 4
HALO = 8

NT_DIMS = (((1,), (1,)), ((), ()))
TN_DIMS = (((0,), (0,)), ((), ()))


def _dot(a, b):
    return jnp.dot(a, b, preferred_element_type=F32)


def _rmsnorm(x, g):
    return x * lax.rsqrt(jnp.mean(x * x, axis=-1, keepdims=True) + EPS) * g


def _silu(x):
    return x * (1.0 / (1.0 + jnp.exp(-x)))


def _params(*semantics):
    return pltpu.CompilerParams(dimension_semantics=semantics, vmem_limit_bytes=VMEM_LIMIT_BYTES)


def _const_spec(shape):
    zeros = (0,) * len(shape)
    return pl.BlockSpec(shape, lambda *_: zeros, pipeline_mode=pl.Buffered(1))


def _adaln_kernel(c_ref, w_ref, b_ref, o_ref):
    a = _silu(c_ref[...])
    o_ref[...] = _dot(a.astype(BF16), w_ref[0].astype(BF16)) + b_ref[...]


def _adaln(cs, w, b, layer):
    rows, d = cs.shape
    n = w.shape[2]
    tn = 1024
    return pl.pallas_call(
        _adaln_kernel,
        out_shape=jax.ShapeDtypeStruct((rows, n), F32),
        grid=(n // tn,),
        in_specs=[pl.BlockSpec((rows, d), lambda j: (0, 0)),
                  pl.BlockSpec((1, d, tn), lambda j: (layer, 0, j)),
                  pl.BlockSpec((1, tn), lambda j: (0, j))],
        out_specs=pl.BlockSpec((rows, tn), lambda j: (0, j)),
        compiler_params=_params("parallel"),
        name="adaln",
    )(cs, w, b)


COL_CQ = 0
COL_CKV = 256
COL_KPE = 512
COL_KPE_ROT = 640
COL_RQ = 768
COL_RK = 1280
COL_RV = 1792
COL_RG = 2304
COLS_IN = 2816
RET_W = RET_HEADS * RET_DK


def _preproj_kernel(with_queries, x_ref, sh_ref, sc_ref, g1_ref, win_ref, gq_ref, wuq_ref,
                    gkv_ref, wk_ref, wv_ref, cm_ref, sm_ref, cmt_ref, smt_ref, cr_ref, sr_ref, *out_refs):
    if with_queries:
        q_ref, k_ref, v_ref, rq_ref, rk_ref, rv_ref, sg_ref = out_refs
    else:
        k_ref, v_ref, rk_ref, rv_ref = out_refs
    n_sub = max(1, min(PRE_SUBTILES, x_ref.shape[1] // MXU_DIM))
    rp = x_ref.shape[1] // n_sub
    gain = g1_ref[...] * (1.0 + sc_ref[0])
    shift = sh_ref[0]

    projs = []
    for r in range(n_sub):
        rows = slice(r * rp, (r + 1) * rp)
        h = _rmsnorm(x_ref[0, rows, :], gain) + shift
        projs.append(_dot(h.astype(BF16), win_ref[...]))

    for r in range(n_sub):
        rows = slice(r * rp, (r + 1) * rp)
        proj = projs[r]
        cm = cm_ref[rows, :]
        sm = sm_ref[rows, :]
        cr = cr_ref[rows, :]
        sr = sr_ref[rows, :]

        ckvn = _rmsnorm(proj[:, COL_CKV:COL_CKV + 256], gkv_ref[...]).astype(BF16)
        kn = _dot(ckvn, wk_ref[...])
        vt = lax.dot_general(wv_ref[...], ckvn, NT_DIMS, preferred_element_type=F32).astype(BF16)
        ones_rows = jnp.ones((BF16_ROWS, rp), BF16)
        for hd in range(MLA_HEADS):
            v_ref[0, hd * VT_ROWS:hd * VT_ROWS + MLA_V, rows] = vt[hd * MLA_V:(hd + 1) * MLA_V]
            v_ref[0, hd * VT_ROWS + MLA_V:(hd + 1) * VT_ROWS, rows] = ones_rows
        kpe = proj[:, COL_KPE:COL_KPE + LANES] * cm + proj[:, COL_KPE_ROT:COL_KPE_ROT + LANES] * sm
        for hd in range(MLA_HEADS):
            sl = slice(hd * HEAD_PAD, (hd + 1) * HEAD_PAD)
            k_ref[0, rows, sl] = (kn[:, sl] + kpe).astype(BF16)

        for hd in range(RET_HEADS):
            sl = slice(hd * RET_DK, (hd + 1) * RET_DK)
            b = proj[:, COL_RK + hd * RET_DK:COL_RK + (hd + 1) * RET_DK]
            rk_ref[0, rows, sl] = ((b * cr + pltpu.roll(b, RET_DK // 2, 1) * sr) * (RET_DK ** -0.5)).astype(BF16)
        rv_ref[0, rows, :] = proj[:, COL_RV:COL_RV + RET_W].astype(BF16)

        if with_queries:
            cqn = _rmsnorm(proj[:, COL_CQ:COL_CQ + 256], gq_ref[...]).astype(BF16)
            qa = lax.dot_general(wuq_ref[...], cqn, NT_DIMS, preferred_element_type=F32)
            cmt = cmt_ref[:, rows]
            smt = smt_ref[:, rows]
            rot0 = MLA_HEADS * HEAD_PAD
            for hd in range(MLA_HEADS):
                sl = slice(hd * HEAD_PAD, (hd + 1) * HEAD_PAD)
                a = qa[sl, :]
                b = qa[rot0 + hd * HEAD_PAD:rot0 + (hd + 1) * HEAD_PAD, :]
                q_ref[0, sl, rows] = ((a * cmt + b * smt) * Q_SCALE).astype(BF16)
            for hd in range(RET_HEADS):
                sl = slice(hd * RET_DK, (hd + 1) * RET_DK)
                a = proj[:, COL_RQ + hd * RET_DK:COL_RQ + (hd + 1) * RET_DK]
                rq_ref[0, rows, sl] = (a * cr + pltpu.roll(a, RET_DK // 2, 1) * sr).astype(BF16)
            sg_ref[0, rows, :] = _silu(proj[:, COL_RG:COL_RG + RET_W]).astype(BF16)


def _preproj(x, sh, sc, g1, win, gq, wuq, gkv, wk, wv, cm, sm, cr, sr, *, with_queries):
    bsz, seq, d = x.shape
    tm = min(PRE_TM, seq)
    qk_w = MLA_HEADS * HEAD_PAD
    v_w = MLA_HEADS * VT_ROWS
    cmt, smt = cm.T, sm.T
    tabt_spec = pl.BlockSpec((LANES, tm), lambda b, i: (0, i))
    qt_shape = jax.ShapeDtypeStruct((bsz, qk_w, seq), BF16)
    qt_spec = pl.BlockSpec((1, qk_w, tm), lambda b, i: (b, 0, i))

    def tok(width):
        return pl.BlockSpec((1, tm, width), lambda b, i: (b, i, 0))

    def out(width):
        return jax.ShapeDtypeStruct((bsz, seq, width), BF16)

    mod_spec = pl.BlockSpec((1, 1, d), lambda b, i: (b, 0, 0))
    tab_spec = pl.BlockSpec((tm, LANES), lambda b, i: (i, 0))
    vt_shape = jax.ShapeDtypeStruct((bsz, v_w, seq), BF16)
    vt_spec = pl.BlockSpec((1, v_w, tm), lambda b, i: (b, 0, i))
    if with_queries:
        out_shape = [qt_shape, out(qk_w), vt_shape, out(RET_W), out(RET_W), out(RET_W), out(RET_W)]
        out_specs = [qt_spec, tok(qk_w), vt_spec, tok(RET_W), tok(RET_W), tok(RET_W), tok(RET_W)]
    else:
        out_shape = [out(qk_w), vt_shape, out(RET_W), out(RET_W)]
        out_specs = [tok(qk_w), vt_spec, tok(RET_W), tok(RET_W)]
    return pl.pallas_call(
        functools.partial(_preproj_kernel, with_queries),
        out_shape=out_shape,
        grid=(bsz, seq // tm),
        in_specs=[tok(d), mod_spec, mod_spec, _const_spec(g1.shape), _const_spec(win.shape),
                  _const_spec(gq.shape), _const_spec(wuq.shape), _const_spec(gkv.shape),
                  _const_spec(wk.shape), _const_spec(wv.shape), tab_spec, tab_spec, tabt_spec, tabt_spec,
                  tab_spec, tab_spec],
        out_specs=out_specs,
        compiler_params=_params("parallel", "parallel"),
        name="preproj_q" if with_queries else "preproj_ctx",
    )(x, sh, sc, g1, win, gq, wuq, gkv, wk, wv, cm, sm, cmt, smt, cr, sr)


def _mla_attn_kernel(q_ref, kl_ref, kc_ref, vl_ref, vc_ref, o_ref, s_ref):
    lat_len, ctx_len = kl_ref.shape[1], kc_ref.shape[1]
    kb_ctx = min(ATT_KB, ctx_len)
    blocks = ([(False, k0, ATT_KB) for k0 in range(0, lat_len, ATT_KB)]
              + [(True, k0, kb_ctx) for k0 in range(0, ctx_len, kb_ctx)])
    n_blk = len(blocks)
    pair_rows = 2 * MLA_V

    def s_rows(b):
        from_ctx, k0, size = blocks[b]
        k0 += lat_len if from_ctx else 0
        return slice(k0, k0 + size)

    def k_block(h, b):
        from_ctx, k0, size = blocks[b]
        ref = kc_ref if from_ctx else kl_ref
        return ref[0, k0:k0 + size, h * HEAD_PAD:(h + 1) * HEAD_PAD]

    def vt_block(h, b):
        from_ctx, k0, size = blocks[b]
        ref = vc_ref if from_ctx else vl_ref
        return ref[0, h * VT_ROWS:(h + 1) * VT_ROWS, k0:k0 + size]

    tq = s_ref.shape[2]
    n_rows = (q_ref.shape[2] // tq) * MLA_HEADS
    n_units = n_rows * n_blk
    exp_lag = n_blk
    pv_lag = exp_lag + ATT_PV_LAG
    m_run = [None] * n_rows
    ot = [None] * n_rows
    p_blocks = {}
    for t in range(n_units + pv_lag):
        bm = None
        if t < n_units:
            r, b = divmod(t, n_blk)
            qi, h = divmod(r, MLA_HEADS)
            qt = q_ref[0, h * HEAD_PAD:(h + 1) * HEAD_PAD, qi * tq:(qi + 1) * tq]
            s = _dot(k_block(h, b), qt)
            s_ref[r % 2, s_rows(b), :] = s
            bm = jnp.max(s, axis=0, keepdims=True)
            m_run[r] = bm if m_run[r] is None else jnp.maximum(m_run[r], bm)
        e = t - exp_lag
        if 0 <= e < n_units:
            r, b = divmod(e, n_blk)
            m_use = m_run[r]
            if bm is not None:
                m_use = jnp.maximum(m_use, jnp.minimum(bm, SCORE_FLOOR))
            p_blocks[e] = jnp.exp2(s_ref[r % 2, s_rows(b), :] - m_use).astype(BF16)
        v = t - pv_lag
        if 0 <= v < n_units:
            r, b = divmod(v, n_blk)
            qi, h = divmod(r, MLA_HEADS)
            pv = _dot(vt_block(h, b), p_blocks.pop(v))
            ot[r] = pv if ot[r] is None else ot[r] + pv
            if b == n_blk - 1:
                o_h = ot[r][:MLA_V] * (1.0 / ot[r][MLA_V:MLA_V + 1])
                ot[r] = o_h
                if h % 2 == 1:
                    pair_t = jnp.concatenate([ot[r - 1], o_h], axis=0)
                    cols = slice((h // 2) * pair_rows, (h // 2 + 1) * pair_rows)
                    o_ref[0, qi * tq:(qi + 1) * tq, cols] = pair_t.T.astype(BF16)
                    ot[r - 1] = ot[r] = None


def _mla_attn(qt, k_lat, k_ctx, vt_lat, vt_ctx):
    bsz, qk_w, seq = qt.shape
    ctx_len = k_ctx.shape[1]
    v_w = vt_lat.shape[1]
    o_w = MLA_HEADS * MLA_V
    tq = min(ATT_TQ, seq)
    tstep = min(ATT_TQ * ATT_SUBTILES, seq)
    assert seq % ATT_KB == 0 and ctx_len % min(ATT_KB, ctx_len) == 0 and tstep % tq == 0
    return pl.pallas_call(
        _mla_attn_kernel,
        out_shape=jax.ShapeDtypeStruct((bsz, seq, o_w), BF16),
        grid=(bsz, seq // tstep),
        in_specs=[pl.BlockSpec((1, qk_w, tstep), lambda b, i: (b, 0, i)),
                  pl.BlockSpec((1, seq, qk_w), lambda b, i: (b, 0, 0)),
                  pl.BlockSpec((1, ctx_len, qk_w), lambda b, i: (b, 0, 0)),
                  pl.BlockSpec((1, v_w, seq), lambda b, i: (b, 0, 0)),
                  pl.BlockSpec((1, v_w, ctx_len), lambda b, i: (b, 0, 0))],
        out_specs=pl.BlockSpec((1, tstep, o_w), lambda b, i: (b, i, 0)),
        scratch_shapes=[pltpu.VMEM((2, seq + ctx_len, tq), F32)],
        compiler_params=_params("parallel", "parallel"),
        name="mla_attn",
    )(qt, k_lat, k_ctx, vt_lat, vt_ctx)


def _retention_kernel(rd_ref, q_ref, k_ref, v_ref, sg_ref, kc_ref, vc_ref, g_ref, o_ref, pf_ref, sb_ref,
                      tab_ref, gc_ref, dm_ref):
    c = RET_CHUNK
    n_chunks = q_ref.shape[1] // c

    @pl.when(pl.program_id(0) == 0)
    def _():
        row = lax.broadcasted_iota(jnp.int32, (c, RET_DV), 0).astype(F32)
        diff = (lax.broadcasted_iota(jnp.int32, (c, c), 0)
                - lax.broadcasted_iota(jnp.int32, (c, c), 1)).astype(F32)
        for hd in range(RET_HEADS):
            lgf = -jnp.exp(rd_ref[hd, 0:1, :])
            lgb = -jnp.exp(rd_ref[hd, 1:2, :])
            lgf1 = lgf[:, :RET_DV]
            lgb1 = lgb[:, :RET_DV]
            tab_ref[hd, 0] = jnp.exp(lgf1 * (c - 1.0 - row))
            tab_ref[hd, 1] = jnp.exp(lgb1 * row)
            tab_ref[hd, 2] = jnp.exp(lgf1 * (row + 1.0))
            tab_ref[hd, 3] = jnp.exp(lgb1 * (c - row))
            gc_ref[hd, 0] = jnp.broadcast_to(jnp.exp(lgf1 * float(c)), (8, RET_DV))
            gc_ref[hd, 1] = jnp.broadcast_to(jnp.exp(lgb1 * float(c)), (8, RET_DV))
            dm_ref[hd] = jnp.where(diff >= 0.0, jnp.exp(lgf * jnp.maximum(diff, 0.0)),
                                   jnp.exp(lgb * jnp.maximum(-diff, 0.0)))

    for hd in range(RET_HEADS):
        hs = slice(hd * RET_DK, (hd + 1) * RET_DK)
        zeta_f, zeta_b, xi_f, xi_b = tab_ref[hd, 0], tab_ref[hd, 1], tab_ref[hd, 2], tab_ref[hd, 3]
        gc_f = gc_ref[hd, 0, 0:1, :]
        gc_b = gc_ref[hd, 1, 0:1, :]
        dmat = dm_ref[hd]

        def kv_outer(kk, vv, zeta_f=zeta_f, zeta_b=zeta_b):
            vf = vv.astype(F32)
            v2 = jnp.concatenate([vf * zeta_f, vf * zeta_b], axis=1).astype(BF16)
            return lax.dot_general(kk, v2, TN_DIMS, preferred_element_type=F32)

        p_ctx = kv_outer(kc_ref[0, :, hs], vc_ref[0, :, hs])
        sf = p_ctx[:, :RET_DV]
        sb = p_ctx[:, RET_DV:]

        for ci in reversed(range(n_chunks)):
            rows = slice(ci * c, (ci + 1) * c)
            p = kv_outer(k_ref[0, rows, hs], v_ref[0, rows, hs])
            pf_ref[hd * n_chunks + ci] = p[:, :RET_DV]
            sb_ref[hd * n_chunks + ci] = sb
            sb = gc_b * sb + p[:, RET_DV:]

        g = g_ref[:, hs]
        for ci in range(n_chunks):
            rows = slice(ci * c, (ci + 1) * c)
            qc = q_ref[0, rows, hs]
            s = lax.dot_general(qc, k_ref[0, rows, hs], NT_DIMS, preferred_element_type=F32)
            inner = _dot((s * dmat).astype(BF16), v_ref[0, rows, hs])
            st = jnp.concatenate([sf, sb_ref[hd * n_chunks + ci]], axis=1).astype(BF16)
            cross = _dot(qc, st)
            y = inner + cross[:, :RET_DV] * xi_f + cross[:, RET_DV:] * xi_b
            mu = jnp.mean(y, axis=-1, keepdims=True)
            d = y - mu
            var = jnp.mean(d * d, axis=-1, keepdims=True)
            yn = d * lax.rsqrt(var + EPS) * g
            o_ref[0, rows, hs] = (sg_ref[0, rows, hs].astype(F32) * yn).astype(BF16)
            sf = gc_f * sf + pf_ref[hd * n_chunks + ci]


def _retention(rd, rq, rk, rv, sg, rk_ctx, rv_ctx, g_ret):
    bsz, seq, width = rq.shape
    ctx_len = rk_ctx.shape[1]
    assert ctx_len == RET_CHUNK and seq % RET_CHUNK == 0

    def tokens(rows):
        return pl.BlockSpec((1, rows, width), lambda b: (b, 0, 0))

    n_states = RET_HEADS * (seq // RET_CHUNK)
    return pl.pallas_call(
        _retention_kernel,
        out_shape=jax.ShapeDtypeStruct((bsz, seq, width), BF16),
        grid=(bsz,),
        in_specs=[_const_spec(rd.shape), tokens(seq), tokens(seq), tokens(seq), tokens(seq),
                  tokens(ctx_len), tokens(ctx_len), _const_spec(g_ret.shape)],
        out_specs=tokens(seq),
        scratch_shapes=[pltpu.VMEM((n_states, RET_DK, RET_DV), F32), pltpu.VMEM((n_states, RET_DK, RET_DV), F32),
                        pltpu.VMEM((RET_HEADS, 4, RET_CHUNK, RET_DV), F32), pltpu.VMEM((RET_HEADS, 2, 8, RET_DV), F32),
                        pltpu.VMEM((RET_HEADS, RET_CHUNK, RET_CHUNK), F32)],
        compiler_params=_params("arbitrary"),
        name="retention",
    )(rd, rq, rk, rv, sg, rk_ctx, rv_ctx, g_ret)


def _ffn_kernel(x_ref, xp_ref, xn_ref, om_ref, omp_ref, omn_ref, or_ref, orp_ref, orn_ref,
                gt1_ref, sh_ref, sc_ref, gt_ref, g2_ref, gf_ref, wo_ref,
                wup_ref, cw_ref, cb_ref, wdn_ref, o_ref, hs_ref, os_ref, acc_ref):
    tm = x_ref.shape[1]
    n_inner = wup_ref.shape[0]
    half = om_ref.shape[2]
    i = pl.program_id(1)

    def edge(before_ref, after_ref):
        return jnp.concatenate([before_ref[0].astype(F32)[BF16_ROWS - HALO:], after_ref[0].astype(F32)[:HALO]],
                               axis=0).astype(BF16)

    gt1 = gt1_ref[0]
    gain = g2_ref[...] * (1.0 + sc_ref[0])
    shift = sh_ref[0]
    has_prev = (i > 0).astype(F32)
    has_next = (i < pl.num_programs(1) - 1).astype(F32)
    row8 = lax.broadcasted_iota(jnp.int32, (HALO, 2 * FFN_FC), 0)

    n_sub = FFN_SUBTILES
    rp = tm // n_sub
    last = n_sub - 1

    def up_proj(j, r):
        rows = slice(r * rp, (r + 1) * rp + (2 * HALO if r == last else 0))
        return _dot(hs_ref[rows, :], wup_ref[j])

    def conv_act(u, j, r):
        um = u[r][0:rp]
        if r == 0:
            u_before = u[last][rp + HALO - 1:rp + HALO] * has_prev
        else:
            u_before = u[r - 1][rp - 1:rp]
        if r == last:
            u_after = u[last][rp + HALO:rp + HALO + 1] * has_next
        else:
            u_after = u[r + 1][0:1]
        up = pltpu.roll(um, 1, 0)
        up = jnp.concatenate([jnp.where(row8 == 0, u_before, up[0:HALO]), up[HALO:]], axis=0)
        un = pltpu.roll(um, rp - 1, 0)
        un = jnp.concatenate([un[:rp - HALO], jnp.where(row8 == HALO - 1, u_after, un[rp - HALO:])], axis=0)
        cw = cw_ref[j]
        cv = up * cw[0:1] + um * cw[1:2] + un * cw[2:3] + cb_ref[j]
        return (_silu(cv[:, FFN_FC:]) * cv[:, :FFN_FC]).astype(BF16)

    os_ref[0:tm, 0:half] = om_ref[0]
    os_ref[0:tm, half:] = or_ref[0]
    os_ref[tm:, 0:half] = edge(omp_ref, omn_ref)
    os_ref[tm:, half:] = edge(orp_ref, orn_ref)
    mixed = _dot(os_ref[...], wo_ref[...])
    x = x_ref[0] + gt1 * mixed[0:tm]
    o_ref[0] = x
    halo = jnp.concatenate([xp_ref[0], xn_ref[0]], axis=0) + gt1 * mixed[tm:]
    hs_ref[tm:tm + 2 * HALO, :] = (_rmsnorm(halo, gain) + shift).astype(BF16)
    u = []
    for r in range(n_sub):
        rows = slice(r * rp, (r + 1) * rp)
        hs_ref[rows, :] = (_rmsnorm(x[rows], gain) + shift).astype(BF16)
        u.append(up_proj(0, r))

    pending = [None] * n_sub
    acc_live = [False] * n_sub
    for j in range(n_inner):
        u_next = [None] * n_sub
        for r in range(n_sub):
            rows = slice(r * rp, (r + 1) * rp)
            if j + 1 < n_inner:
                u_next[r] = up_proj(j + 1, r)
            part = _dot(conv_act(u, j, r), wdn_ref[j])
            if pending[r] is None and j + 1 < n_inner:
                pending[r] = part
                continue
            total = part if pending[r] is None else pending[r] + part
            pending[r] = None
            if acc_live[r]:
                total = acc_ref[rows, :] + total
            if j + 1 < n_inner:
                acc_ref[rows, :] = total
                acc_live[r] = True
            else:
                o_ref[0, rows, :] = _rmsnorm(o_ref[0, rows, :] + gt_ref[0] * total, gf_ref[...])
        u = u_next


def _ffn(x, o_mla, o_ret, gt1, sh, sc, gt, g2, gf, wo, wup, cw, cb, wdn):
    bsz, seq, d = x.shape
    tm = min(FFN_TM, seq)
    mod_spec = pl.BlockSpec((1, 1, d), lambda b, i: (b, 0, 0))

    def tile(width):
        return pl.BlockSpec((1, tm, width), lambda b, i: (b, i, 0))

    def before(rows, width):
        return pl.BlockSpec((1, rows, width), lambda b, i: (b, jnp.maximum(i * (tm // rows) - 1, 0), 0))

    def after(rows, width):
        last = seq // rows - 1
        return pl.BlockSpec((1, rows, width), lambda b, i: (b, jnp.minimum((i + 1) * (tm // rows), last), 0))

    mw, rw = o_mla.shape[2], o_ret.shape[2]
    return pl.pallas_call(
        _ffn_kernel,
        out_shape=jax.ShapeDtypeStruct((bsz, seq, d), F32),
        grid=(bsz, seq // tm),
        in_specs=[tile(d), before(HALO, d), after(HALO, d),
                  tile(mw), before(BF16_ROWS, mw), after(BF16_ROWS, mw),
                  tile(rw), before(BF16_ROWS, rw), after(BF16_ROWS, rw),
                  mod_spec, mod_spec, mod_spec, mod_spec, _const_spec(g2.shape), _const_spec(gf.shape),
                  _const_spec(wo.shape),
                  _const_spec(wup.shape), _const_spec(cw.shape), _const_spec(cb.shape), _const_spec(wdn.shape)],
        out_specs=tile(d),
        scratch_shapes=[pltpu.VMEM((tm + 2 * HALO, d), BF16), pltpu.VMEM((tm + 2 * HALO, mw + rw), BF16),
                        pltpu.VMEM((tm, d), F32)],
        compiler_params=_params("parallel", "parallel"),
        name="ffn",
    )(x, x, x, o_mla, o_mla, o_mla, o_ret, o_ret, o_ret, gt1, sh, sc, gt, g2, gf, wo, wup, cw, cb, wdn)


def _rot_half_cols(w, half):
    shape = w.shape
    w = w.reshape(shape[:-1] + (shape[-1] // (2 * half), 2, half))
    return jnp.stack([-w[..., 1, :], w[..., 0, :]], axis=-2).reshape(shape)


def _rope_angles(pos, dim):
    inv = ROPE_BASE ** (-np.arange(0, dim, 2, dtype=np.float64) / dim)
    return pos.astype(np.float64)[:, None] * inv[None, :]


def _mla_tables(seq, with_pos):
    if not with_pos:
        return np.ones((seq, LANES), np.float32), np.zeros((seq, LANES), np.float32)
    pos = np.arange(seq)
    quarter = MLA_ROPE // 2
    ar = _rope_angles(pos // GRID_W, quarter)
    ac = _rope_angles(pos % GRID_W, quarter)
    ang = np.concatenate([ar, ar, ac, ac], axis=-1)
    pad = LANES - MLA_NOPE - MLA_ROPE
    cos = np.concatenate([np.ones((seq, MLA_NOPE)), np.cos(ang), np.ones((seq, pad))], axis=-1)
    sin = np.concatenate([np.zeros((seq, MLA_NOPE)), np.sin(ang), np.zeros((seq, pad))], axis=-1)
    return cos.astype(np.float32), sin.astype(np.float32)


def _ret_tables(seq, with_pos):
    if not with_pos:
        return np.ones((seq, RET_DK), np.float32), np.zeros((seq, RET_DK), np.float32)
    ang = _rope_angles(np.arange(seq), RET_DK)
    return (np.concatenate([np.cos(ang), np.cos(ang)], axis=-1).astype(np.float32),
            np.concatenate([-np.sin(ang), np.sin(ang)], axis=-1).astype(np.float32))


def _pad_heads(nope, rope):
    r = nope.shape[0]
    pad = jnp.zeros((r, MLA_HEADS, HEAD_PAD - MLA_NOPE - MLA_ROPE), nope.dtype)
    return jnp.concatenate([nope, rope, pad], axis=-1).reshape(r, MLA_HEADS * HEAD_PAD)


def _pair_up_kernel(a_ref, g_ref, o_ref):
    o_ref[0, :, :FFN_FC] = a_ref[0].astype(BF16)
    o_ref[0, :, FFN_FC:] = g_ref[0].astype(BF16)


def _pair_up_weights(w_up, layer, nj):
    d = w_up.shape[1]
    return pl.pallas_call(
        _pair_up_kernel,
        out_shape=jax.ShapeDtypeStruct((nj, d, 2 * FFN_FC), BF16),
        grid=(nj,),
        in_specs=[pl.BlockSpec((1, d, FFN_FC), lambda j: (layer, 0, j)),
                  pl.BlockSpec((1, d, FFN_FC), lambda j: (layer, 0, nj + j))],
        out_specs=pl.BlockSpec((1, d, 2 * FFN_FC), lambda j: (j, 0, 0)),
        compiler_params=_params("parallel"),
        name="pair_up_weights",
    )(w_up, w_up)


def _rope_placement():
    place = np.zeros((LANES, LANES), np.float32)
    rot = np.zeros((LANES, LANES), np.float32)
    half = MLA_ROPE // 4
    for c in range(MLA_ROPE):
        place[c, MLA_NOPE + c] = 1.0
        g, w = divmod(c, 2 * half)
        if w < half:
            rot[g * 2 * half + w + half, MLA_NOPE + c] = -1.0
        else:
            rot[g * 2 * half + w - half, MLA_NOPE + c] = 1.0
    return place, rot


def _win_kernel(w_ref, place_ref, rot_ref, o_ref):
    w = w_ref[0]
    o_ref[:, 0:COL_KPE] = w[:, 0:COL_KPE].astype(BF16)
    blk = w[:, COL_KPE:COL_KPE + LANES].astype(BF16)
    o_ref[:, COL_KPE:COL_KPE + LANES] = _dot(blk, place_ref[...]).astype(BF16)
    o_ref[:, COL_KPE_ROT:COL_KPE_ROT + LANES] = _dot(blk, rot_ref[...]).astype(BF16)
    o_ref[:, COL_RQ:] = w[:, COL_KPE + MLA_ROPE:].astype(BF16)


def _layout_w_in(w_in, layer):
    d, n = w_in.shape[1], w_in.shape[2]
    assert n - MLA_ROPE + 2 * LANES == COLS_IN
    rows = 256
    place, rot = _rope_placement()
    return pl.pallas_call(
        _win_kernel,
        out_shape=jax.ShapeDtypeStruct((d, COLS_IN), BF16),
        grid=(d // rows,),
        in_specs=[pl.BlockSpec((1, rows, n), lambda i: (layer, i, 0)),
                  _const_spec(place.shape), _const_spec(rot.shape)],
        out_specs=pl.BlockSpec((rows, COLS_IN), lambda i: (i, 0)),
        compiler_params=_params("parallel"),
        name="layout_w_in",
    )(w_in, place.astype(BF16), rot.astype(BF16))


def _layout_weights(w_in_stack, w_uq, w_ukv, w_up_stack, layer, conv_w, conv_b, w_down):
    q_rank = w_uq.shape[0]
    rot_half = MLA_ROPE // 4
    win = _layout_w_in(w_in_stack, layer)

    uq = w_uq.reshape(q_rank, MLA_HEADS, MLA_NOPE + MLA_ROPE)
    uq_n, uq_r = uq[..., :MLA_NOPE], uq[..., MLA_NOPE:]
    wuq = jnp.concatenate([_pad_heads(uq_n, uq_r),
                           _pad_heads(jnp.zeros_like(uq_n), _rot_half_cols(uq_r, rot_half))],
                          axis=-1).T.astype(BF16)

    ukv =w_ukv.reshape(w_ukv.shape[0], MLA_HEADS, MLA_NOPE + MLA_V)
    wk = _pad_heads(ukv[..., :MLA_NOPE], jnp.zeros(ukv.shape[:2] + (MLA_ROPE,), ukv.dtype)).astype(BF16)
    wv = ukv[..., MLA_NOPE:].reshape(w_ukv.shape[0], MLA_HEADS * MLA_V).T.astype(BF16)

    d_ff = w_down.shape[0]
    nj = d_ff // FFN_FC

    def pair_chunks(a):
        lead = a.shape[:-1]
        a = a.reshape(lead + (2, nj, FFN_FC))
        a = jnp.moveaxis(a, -2, 0)
        return a.reshape((nj,) + lead + (2 * FFN_FC,))

    wup = _pair_up_weights(w_up_stack, layer, nj)
    cw = pair_chunks(conv_w)
    cb = pair_chunks(conv_b[None, :])
    wdn = w_down.reshape(nj, FFN_FC, w_down.shape[1]).astype(BF16)
    return win, wuq, wk, wv, wup, cw, cb, wdn


def kernel(x, c, ctx, c_ctx, w_ada, b_ada, g_norm1, w_in, g_q, w_uq, g_kv, w_ukv, ret_decay, g_ret,
           w_out, g_norm2, w_up, conv_w, conv_b, w_down, g_final):
    bsz, seq, d = x.shape
    ctx_len = ctx.shape[1]
    depth = w_ada.shape[0]
    assert depth == 1, "single-layer block"
    l = 0

    rows = -(-(bsz + 1) // 8) * 8
    cs = jnp.concatenate([c, c_ctx[None, :], jnp.zeros((rows - bsz - 1, d), F32)], axis=0)
    mod = _adaln(cs, w_ada, b_ada[l][None, :], l)
    sh1, sc1, gt1, sh2, sc2, gt2 = [mod[:bsz, t * d:(t + 1) * d][:, None, :] for t in range(6)]
    shc1 = jnp.broadcast_to(mod[bsz, 0:d][None, None, :], (bsz, 1, d))
    scc1 = jnp.broadcast_to(mod[bsz, d:2 * d][None, None, :], (bsz, 1, d))

    win, wuq, wk, wv, wup, cw, cb, wdn = _layout_weights(w_in, w_uq[l], w_ukv[l], w_up, l, conv_w[l],
                                                       conv_b[l], w_down[l])
    g1 = g_norm1[l][None, :]
    gq = g_q[l][None, :]
    gkv = g_kv[l][None, :]

    cm, sm = _mla_tables(seq, True)
    cr, sr = _ret_tables(seq, True)
    q, k, v, rq, rk, rv, sg = _preproj(x, sh1, sc1, g1, win, gq, wuq, gkv, wk, wv, cm, sm, cr, sr,
                                       with_queries=True)
    cm0, sm0 = _mla_tables(ctx_len, False)
    cr0, sr0 = _ret_tables(ctx_len, False)
    k_c, v_c, rk_c, rv_c = _preproj(ctx, shc1, scc1, g1, win, gq, wuq, gkv, wk, wv, cm0, sm0, cr0, sr0,
                                    with_queries=False)

    o_mla = _mla_attn(q, k, k_c, v, v_c)
    rd = jnp.broadcast_to(jnp.transpose(ret_decay[l])[:, :, None], (RET_HEADS, 2, RET_CHUNK)).astype(F32)
    o_ret = _retention(rd, rq, rk, rv, sg, rk_c, rv_c, g_ret[l][None, :])

    return _ffn(x, o_mla, o_ret, gt1, sh2, sc2, gt2, g_norm2[l][None, :], g_final[None, :],
                w_out[l].astype(BF16), wup, cw, cb, wdn)
```
